```python
import math
import jax, jax.numpy as jnp
from jax import lax
import numpy as np

D_MODEL = 1024
BATCH = 32
SEQ = 2048
DEPTH = 1

HEAD_DIM = 64
N_Q_HEADS = D_MODEL // HEAD_DIM
N_KV_HEADS = N_Q_HEADS // 8
Q_PER_KV = N_Q_HEADS // N_KV_HEADS
WINDOW = 128
ATTN_BLOCK = 128
NUM_BUCKETS = 32
MAX_DISTANCE = 128
D_INNER = 2 * D_MODEL
SSM_HEAD_DIM = 64
N_SSM_HEADS = D_INNER // SSM_HEAD_DIM
N_SSM_GROUPS = 4
HEADS_PER_GROUP = N_SSM_HEADS // N_SSM_GROUPS
D_STATE = 128
CONV_WIDTH = 4
CHUNK = 128
CONV_CH = D_INNER + 2 * N_SSM_GROUPS * D_STATE
N_EXPERT_GROUPS = 4
EXPERTS_PER_GROUP = 8
N_EXPERTS = N_EXPERT_GROUPS * EXPERTS_PER_GROUP
TOP_K = 2
D_EXPERT = D_MODEL // 2
MOE_BLOCK = 128
DEEPNORM_ALPHA = (2.0 * DEPTH) ** 0.25
DEEPNORM_BETA = (8.0 * DEPTH) ** -0.25
LN_EPS = 1e-5
PROJ_SIZES = (N_Q_HEADS * HEAD_DIM, N_KV_HEADS * HEAD_DIM, N_KV_HEADS * HEAD_DIM,
              D_INNER, D_INNER, N_SSM_GROUPS * D_STATE, N_SSM_GROUPS * D_STATE,
              N_SSM_HEADS, D_MODEL, D_MODEL)
PROJ_COLS = sum(PROJ_SIZES)
PROJ_SPLITS = tuple(int(v) for v in np.cumsum(PROJ_SIZES)[:-1])

kernel_name = "hybrid_swa_sink_ssd_hmoe_deepnorm"


def layer_norm(x, g, b):
    xf = x.astype(jnp.float32)
    mu = jnp.mean(xf, -1, keepdims=True)
    var = jnp.mean(jnp.square(xf - mu), -1, keepdims=True)
    y = (xf - mu) * lax.rsqrt(var + LN_EPS) * g.astype(jnp.float32) + b.astype(jnp.float32)
    return y.astype(x.dtype)


def t5_causal_bucket(dist):
    max_exact = NUM_BUCKETS // 2
    d_f = jnp.maximum(dist, 1).astype(jnp.float32)
    large = max_exact + (jnp.log(d_f / max_exact) / math.log(MAX_DISTANCE / max_exact)
                         * (NUM_BUCKETS - max_exact)).astype(jnp.int32)
    large = jnp.minimum(large, NUM_BUCKETS - 1)
    return jnp.where(dist < max_exact, dist, large)


def banded_relative_bias(table):
    qi = jnp.arange(ATTN_BLOCK)[:, None]
    kj = jnp.arange(2 * ATTN_BLOCK)[None, :]
    dist = qi + ATTN_BLOCK - kj
    bias = table[t5_causal_bucket(jnp.clip(dist, 0, None))].astype(jnp.float32)
    return jnp.transpose(bias, (2, 0, 1)).reshape(N_KV_HEADS, Q_PER_KV, ATTN_BLOCK, 2 * ATTN_BLOCK)


def sliding_window_attention(q, k, v, sink, pos_bias):
    b, s = q.shape[:2]
    nb = s // ATTN_BLOCK
    scale = HEAD_DIM ** -0.5
    qb = q.reshape(b, nb, ATTN_BLOCK, N_KV_HEADS, Q_PER_KV, HEAD_DIM)

    def band(t):
        tb = t.reshape(b, nb, ATTN_BLOCK, N_KV_HEADS, HEAD_DIM)
        prev = jnp.concatenate([jnp.zeros_like(tb[:, :1]), tb[:, :-1]], axis=1)
        return jnp.concatenate([prev, tb], axis=2)

    kw, vw = band(k), band(v)
    qi = jnp.arange(ATTN_BLOCK)[:, None]
    kj = jnp.arange(2 * ATTN_BLOCK)[None, :]
    dist = qi + ATTN_BLOCK - kj
    in_window = (dist >= 0) & (dist < WINDOW)
    sink_b = sink.astype(jnp.float32).reshape(N_KV_HEADS, Q_PER_KV)[None, :, :, None, None]

    def one_block(args):
        qblk, kblk, vblk, i = args
        sc = jnp.einsum('bqkgd,bskd->bkgqs', qblk, kblk).astype(jnp.float32) * scale + pos_bias
        key_ok = in_window & ((kj >= ATTN_BLOCK) | (i > 0))
        sc = jnp.where(key_ok, sc, -jnp.inf)
        m = jnp.maximum(jnp.max(sc, -1, keepdims=True), sink_b)
        p = jnp.exp(sc - m)
        denom = jnp.sum(p, -1, keepdims=True) + jnp.exp(sink_b - m)
        return jnp.einsum('bkgqs,bskd->bqkgd', (p / denom).astype(vblk.dtype), vblk)

    out = lax.map(one_block, (jnp.moveaxis(qb, 1, 0), jnp.moveaxis(kw, 1, 0),
                              jnp.moveaxis(vw, 1, 0), jnp.arange(nb)))
    return jnp.moveaxis(out, 0, 1).reshape(b, s, N_Q_HEADS * HEAD_DIM)


def causal_depthwise_conv(x, w, bias):
    y = lax.conv_general_dilated(x, w[:, None, :].astype(x.dtype), window_strides=(1,),
                                 padding=[(CONV_WIDTH - 1, 0)],
                                 dimension_numbers=('NWC', 'WIO', 'NWC'),
                                 feature_group_count=x.shape[-1])
    return y + bias.astype(x.dtype)


def ssd_chunked_scan(xs, dt, a, bm, cm):
    b, s = xs.shape[:2]
    nc = s // CHUNK

    def chunks(t):
        return jnp.moveaxis(t.reshape((b, nc, CHUNK) + t.shape[2:]), 1, 0)

    causal = jnp.tril(jnp.ones((CHUNK, CHUNK), bool))[None, :, :, None, None]

    def step(state, inp):
        xc, dtc, bc, cc = inp
        acum = jnp.cumsum(dtc * a, axis=1)
        seg = acum[:, :, None] - acum[:, None, :]
        decay = jnp.exp(jnp.where(causal, seg, -jnp.inf))
        xdt = xc * dtc[..., None]
        cb = jnp.einsum('btgn,bsgn->btsg', cc, bc)
        y_diag = jnp.einsum('btsg,btsge,bsgep->btgep', cb, decay, xdt)
        y_off = jnp.einsum('btgn,bgepn->btgep', cc, state) * jnp.exp(acum)[..., None]
        to_end = jnp.exp(acum[:, -1:] - acum)
        new_state = (state * jnp.exp(acum[:, -1])[..., None, None]
                     + jnp.einsum('bsgn,bsge,bsgep->bgepn', bc, to_end, xdt))
        return new_state, y_diag + y_off

    state0 = jnp.zeros((b, N_SSM_GROUPS, HEADS_PER_GROUP, SSM_HEAD_DIM, D_STATE), jnp.float32)
    _, y = lax.scan(step, state0, (chunks(xs), chunks(dt), chunks(bm), chunks(cm)))
    return jnp.moveaxis(y, 0, 1).reshape(xs.shape)


def gated_group_rmsnorm(y, z, g):
    b, s = y.shape[:2]
    h = y.astype(jnp.float32) * jax.nn.silu(z.astype(jnp.float32))
    h = h.reshape(b, s, N_SSM_GROUPS, D_INNER // N_SSM_GROUPS)
    h = h * lax.rsqrt(jnp.mean(jnp.square(h), -1, keepdims=True) + LN_EPS)
    return h.reshape(b, s, D_INNER) * g.astype(jnp.float32)


def hybrid_mixer(u, pos_bias, w_in, b_gate, sink, conv_w, conv_b, dt_bias, a_log, d_skip,
                 ssm_norm_g, w_attn_out, w_ssm_out, w_out):
    b, s = u.shape[:2]
    proj = u @ w_in
    q, k, v, z, xs, bm, cm, dt, ga, gs = jnp.split(proj, PROJ_SPLITS, axis=-1)
    attn = sliding_window_attention(q.reshape(b, s, N_KV_HEADS, Q_PER_KV, HEAD_DIM),
                                    k.reshape(b, s, N_KV_HEADS, HEAD_DIM),
                                    v.reshape(b, s, N_KV_HEADS, HEAD_DIM), sink, pos_bias)
    attn_branch = attn @ w_attn_out
    xbc = jax.nn.silu(causal_depthwise_conv(jnp.concatenate([xs, bm, cm], -1), conv_w, conv_b))
    xs, bm, cm = jnp.split(xbc, [D_INNER, D_INNER + N_SSM_GROUPS * D_STATE], axis=-1)
    xs_h = xs.astype(jnp.float32).reshape(b, s, N_SSM_GROUPS, HEADS_PER_GROUP, SSM_HEAD_DIM)
    bm = bm.astype(jnp.float32).reshape(b, s, N_SSM_GROUPS, D_STATE)
    cm = cm.astype(jnp.float32).reshape(b, s, N_SSM_GROUPS, D_STATE)
    dtv = jax.nn.softplus(dt.astype(jnp.float32) + dt_bias.astype(jnp.float32))
    dtv = dtv.reshape(b, s, N_SSM_GROUPS, HEADS_PER_GROUP)
    a = -jnp.exp(a_log.astype(jnp.float32)).reshape(N_SSM_GROUPS, HEADS_PER_GROUP)
    y = ssd_chunked_scan(xs_h, dtv, a, bm, cm)
    y = y + d_skip.astype(jnp.float32).reshape(N_SSM_GROUPS, HEADS_PER_GROUP)[..., None] * xs_h
    y = gated_group_rmsnorm(y.reshape(b, s, D_INNER), z, ssm_norm_g).astype(u.dtype)
    ssm_branch = y @ w_ssm_out
    gate = jax.nn.sigmoid(jnp.concatenate([ga, gs], -1) + b_gate)
    gate_a, gate_s = jnp.split(gate, [D_MODEL], axis=-1)
    return (gate_a * attn_branch + gate_s * ssm_branch) @ w_out


def hierarchical_moe(h, w_group_router, w_expert_router, w_gate_e, w_up_e, w_down_e):
    b, s, d = h.shape
    t = b * s
    xf = h.reshape(t, d)
    g_prob = jax.nn.softmax((xf @ w_group_router).astype(jnp.float32), -1)
    grp = jnp.argmax(g_prob, -1)
    p_grp = jnp.take_along_axis(g_prob, grp[:, None], -1)[:, 0]
    e_logit = (xf @ w_expert_router).astype(jnp.float32).reshape(t, N_EXPERT_GROUPS, EXPERTS_PER_GROUP)
    e_logit = jnp.take_along_axis(e_logit, grp[:, None, None], axis=1)[:, 0]
    top_p, top_i = lax.top_k(jax.nn.softmax(e_logit, -1), TOP_K)
    top_p = top_p / jnp.sum(top_p, -1, keepdims=True)
    gate_w = (p_grp[:, None] * top_p).reshape(-1)
    eid = (grp[:, None] * EXPERTS_PER_GROUP + top_i).reshape(-1).astype(jnp.int32)
    tok = jnp.repeat(jnp.arange(t, dtype=jnp.int32), TOP_K)
    n_assign = t * TOP_K
    order = jnp.argsort(eid)
    se = eid[order]
    counts = jnp.bincount(eid, length=N_EXPERTS).astype(jnp.int32)
    starts = jnp.cumsum(counts) - counts
    padded = (counts + MOE_BLOCK - 1) // MOE_BLOCK * MOE_BLOCK
    pends = jnp.cumsum(padded)
    pstarts = pends - padded
    dest = pstarts[se] + (jnp.arange(n_assign, dtype=jnp.int32) - starts[se])
    n_rows = (n_assign // MOE_BLOCK + N_EXPERTS) * MOE_BLOCK
    row_tok = jnp.full((n_rows,), t, jnp.int32).at[dest].set(tok[order])
    row_w = jnp.zeros((n_rows,), jnp.float32).at[dest].set(gate_w[order])
    n_blk = n_rows // MOE_BLOCK
    block_e = jnp.minimum(jnp.searchsorted(pends, jnp.arange(n_blk, dtype=jnp.int32) * MOE_BLOCK,
                                           side='right'), N_EXPERTS - 1)
    xpad = jnp.concatenate([xf, jnp.zeros((1, d), xf.dtype)], 0)
    xr = xpad[row_tok].reshape(n_blk, MOE_BLOCK, d)

    def expert_block(args):
        xb, e = args
        hid = jax.nn.silu(xb @ w_gate_e[e]) * (xb @ w_up_e[e])
        return hid @ w_down_e[e]

    yr = lax.map(expert_block, (xr, block_e)).reshape(n_rows, d)
    y = jnp.zeros((t + 1, d), yr.dtype).at[row_tok].add(yr * row_w[:, None].astype(yr.dtype))[:t]
    return y.reshape(b, s, d)


def setup_inputs(seed: int = 0) -> dict:
    key = jax.random.key(seed)
    ks = jax.random.split(key, 26)
    f32 = jnp.float32

    def nrm(k, shape, scale):
        return jax.random.normal(k, shape, f32) * scale

    beta = DEEPNORM_BETA
    col_scale = jnp.concatenate([
        jnp.full((sz,), beta if i in (2, 4) else 1.0, f32) for i, sz in enumerate(PROJ_SIZES)])
    dt0 = jnp.exp(jax.random.uniform(ks[9], (DEPTH, N_SSM_HEADS), f32, math.log(1e-3), math.log(1e-1)))
    return {
        "x": nrm(ks[0], (BATCH, SEQ, D_MODEL), 1.0),
        "ln_in_g": 1.0 + nrm(ks[1], (D_MODEL,), 0.02),
        "ln_in_b": nrm(ks[2], (D_MODEL,), 0.02),
        "rel_bias": nrm(ks[3], (NUM_BUCKETS, N_Q_HEADS), 0.2),
        "w_in": nrm(ks[4], (DEPTH, D_MODEL, PROJ_COLS), D_MODEL ** -0.5) * col_scale,
        "b_gate": nrm(ks[5], (DEPTH, 2 * D_MODEL), 0.02),
        "attn_sink": nrm(ks[6], (DEPTH, N_Q_HEADS), 0.5),
        "conv_w": nrm(ks[7], (DEPTH, CONV_WIDTH, CONV_CH), CONV_WIDTH ** -0.5),
        "conv_b": nrm(ks[8], (DEPTH, CONV_CH), 0.02),
        "dt_bias": dt0 + jnp.log(-jnp.expm1(-dt0)),
        "a_log": jnp.log(jax.random.uniform(ks[10], (DEPTH, N_SSM_HEADS), f32, 1.0, 16.0)),
        "d_skip": 1.0 + nrm(ks[11], (DEPTH, N_SSM_HEADS), 0.1),
        "ssm_norm_g": 1.0 + nrm(ks[12], (DEPTH, D_INNER), 0.02),
        "w_attn_out": nrm(ks[13], (DEPTH, N_Q_HEADS * HEAD_DIM, D_MODEL), (N_Q_HEADS * HEAD_DIM) ** -0.5 * beta),
        "w_ssm_out": nrm(ks[14], (DEPTH, D_INNER, D_MODEL), D_INNER ** -0.5 * beta),
        "w_out": nrm(ks[15], (DEPTH, D_MODEL, D_MODEL), D_MODEL ** -0.5 * beta),
        "ln1_g": 1.0 + nrm(ks[16], (DEPTH, D_MODEL), 0.02),
        "ln1_b": nrm(ks[17], (DEPTH, D_MODEL), 0.02),
        "w_group_router": nrm(ks[18], (DEPTH, D_MODEL, N_EXPERT_GROUPS), D_MODEL ** -0.5),
        "w_expert_router": nrm(ks[19], (DEPTH, D_MODEL, N_EXPERTS), D_MODEL ** -0.5),
        "w_gate_e": nrm(ks[20], (DEPTH, N_EXPERTS, D_MODEL, D_EXPERT), D_MODEL ** -0.5 * beta),
        "w_up_e": nrm(ks[21], (DEPTH, N_EXPERTS, D_MODEL, D_EXPERT), D_MODEL ** -0.5 * beta),
        "w_down_e": nrm(ks[22], (DEPTH, N_EXPERTS, D_EXPERT, D_MODEL), D_EXPERT ** -0.5 * beta),
        "ln2_g": 1.0 + nrm(ks[23], (DEPTH, D_MODEL), 0.02),
        "ln2_b": nrm(ks[24], (DEPTH, D_MODEL), 0.02),
    }


def reference(x, ln_in_g, ln_in_b, rel_bias, w_in, b_gate, attn_sink, conv_w, conv_b, dt_bias,
              a_log, d_skip, ssm_norm_g, w_attn_out, w_ssm_out, w_out, ln1_g, ln1_b,
              w_group_router, w_expert_router, w_gate_e, w_up_e, w_down_e, ln2_g, ln2_b):
    h = layer_norm(x, ln_in_g, ln_in_b)
    pos_bias = banded_relative_bias(rel_bias)
    for l in range(DEPTH):
        mix = hybrid_mixer(h, pos_bias, w_in[l], b_gate[l], attn_sink[l], conv_w[l], conv_b[l],
                           dt_bias[l], a_log[l], d_skip[l], ssm_norm_g[l], w_attn_out[l],
                           w_ssm_out[l], w_out[l])
        h = layer_norm(DEEPNORM_ALPHA * h + mix, ln1_g[l], ln1_b[l])
        ffn = hierarchical_moe(h, w_group_router[l], w_expert_router[l], w_gate_e[l], w_up_e[l], w_down_e[l])
        h = layer_norm(DEEPNORM_ALPHA * h + ffn, ln2_g[l], ln2_b[l])
    return h
```

```python
import functools
import math

import jax
import jax.numpy as jnp
from jax import lax
from jax.experimental import pallas as pl
from jax.experimental.pallas import tpu as pltpu

F32 = jnp.float32
BF16 = jnp.bfloat16
I32 = jnp.int32
U32 = jnp.uint32

D_MODEL = 1024
SEQ = 2048
HEAD_DIM = 64
N_Q_HEADS = 16
N_KV_HEADS = 2
ATTN_BLOCK = 128
WINDOW = 128
NUM_BUCKETS = 32
MAX_DISTANCE = 128
D_INNER = 2048
N_SSM_HEADS = 32
N_SSM_GROUPS = 4
D_STATE = 128
CONV_WIDTH = 4
CHUNK = 128
N_EXPERT_GROUPS = 4
EXPERTS_PER_GROUP = 8
N_EXPERTS = 32
D_EXPERT = 512
DEEPNORM_ALPHA = 2.0 ** 0.25
LN_EPS = 1e-5

LANES = 128
SUBLANES = 8

COL_Z = 0
COL_XS = 2048
COL_B = 4096
COL_C = 4608
COL_Q = 5120
COL_GA = 6144
COL_GS = 7168
COL_KV = 8192
N_PROJ = 8704

TM_PROJ = 1024
TN_PROJ = 2176
TM_MIX = 256
TM_FIN = 512
BM_MOE = 128
CONV_CARRY = SUBLANES

VMEM_LIMIT = 56 * 1024 * 1024


def _layer_norm(x, g, b):
    mu = jnp.mean(x, axis=-1, keepdims=True)
    xc = x - mu
    var = jnp.mean(xc * xc, axis=-1, keepdims=True)
    return xc * lax.rsqrt(var + LN_EPS) * g + b


def _ln_inproj_kernel(x_ref, g_ref, b_ref, w_ref, wdt_ref, o_ref, dt_ref, h_scr):
    @pl.when(pl.program_id(1) == 0)
    def _():
        hb = _layer_norm(x_ref[...], g_ref[...], b_ref[...]).astype(BF16)
        h_scr[...] = hb
        dt_ref[...] = jnp.dot(hb, wdt_ref[...], preferred_element_type=F32)

    o_ref[...] = jnp.dot(h_scr[...], w_ref[...], preferred_element_type=F32).astype(BF16)


def _ln_inproj(x2, g, b, w_cat, w_dt):
    t = x2.shape[0]
    return pl.pallas_call(
        _ln_inproj_kernel,
        grid=(t // TM_PROJ, N_PROJ // TN_PROJ),
        in_specs=[
            pl.BlockSpec((TM_PROJ, D_MODEL), lambda i, j: (i, 0)),
            pl.BlockSpec((1, D_MODEL), lambda i, j: (0, 0)),
            pl.BlockSpec((1, D_MODEL), lambda i, j: (0, 0)),
            pl.BlockSpec((D_MODEL, TN_PROJ), lambda i, j: (0, j)),
            pl.BlockSpec((D_MODEL, LANES), lambda i, j: (0, 0)),
        ],
        out_specs=[
            pl.BlockSpec((TM_PROJ, TN_PROJ), lambda i, j: (i, j)),
            pl.BlockSpec((TM_PROJ, LANES), lambda i, j: (i, 0)),
        ],
        out_shape=[
            jax.ShapeDtypeStruct((t, N_PROJ), BF16),
            jax.ShapeDtypeStruct((t, LANES), F32),
        ],
        scratch_shapes=[pltpu.VMEM((TM_PROJ, D_MODEL), BF16)],
        compiler_params=pltpu.CompilerParams(
            dimension_semantics=("arbitrary", "arbitrary"), vmem_limit_bytes=VMEM_LIMIT),
        name="ln_inproj",
    )(x2, g, b, w_cat, w_dt)


def _attn_kernel(sink_ref, q_ref, kvc_ref, kvp_ref, bias_ref, o_ref):
    first = jnp.where(pl.program_id(1) == 0, 1, 0)
    lo = lax.broadcasted_iota(I32, (ATTN_BLOCK, LANES), 1) < HEAD_DIM
    for kv in range(N_KV_HEADS):
        ks = slice(kv * LANES, (kv + 1) * LANES)
        vs = slice((N_KV_HEADS + kv) * LANES, (N_KV_HEADS + kv + 1) * LANES)
        kcat = jnp.concatenate([kvp_ref[:, ks], kvc_ref[:, ks]], axis=0)
        vcat = jnp.concatenate([kvp_ref[:, vs], kvc_ref[:, vs]], axis=0)
        for pp in range(4):
            p = kv * 4 + pp
            qp = q_ref[:, p * LANES:(p + 1) * LANES].astype(F32)
            qm = jnp.concatenate([jnp.where(lo, qp, 0.0), jnp.where(lo, 0.0, qp)], axis=0).astype(BF16)
            s = lax.dot_general(qm, kcat, (((1,), (1,)), ((), ())), preferred_element_type=F32)
            outs = []
            for hh in range(2):
                h = 2 * p + hh
                sh = s[hh * ATTN_BLOCK:(hh + 1) * ATTN_BLOCK] + bias_ref[first, h]
                snk = sink_ref[h]
                m = jnp.maximum(jnp.max(sh, axis=-1, keepdims=True), snk)
                pe = jnp.exp(sh - m)
                den = jnp.sum(pe, axis=-1, keepdims=True) + jnp.exp(snk - m)
                o = jnp.dot(pe.astype(BF16), vcat, preferred_element_type=F32)
                outs.append(o / den)
            o_ref[:, p * LANES:(p + 1) * LANES] = jnp.where(lo, outs[0], outs[1]).astype(BF16)


def _swa_attn(proj, bias2, sink, nbatch):
    t = proj.shape[0]
    nb = SEQ // ATTN_BLOCK
    q_blk = COL_Q // D_MODEL
    kv_blk = COL_KV // 512
    return pl.pallas_call(
        _attn_kernel,
        grid=(nbatch, nb),
        in_specs=[
            pl.BlockSpec(memory_space=pltpu.SMEM),
            pl.BlockSpec((ATTN_BLOCK, D_MODEL), lambda b, i: (b * nb + i, q_blk)),
            pl.BlockSpec((ATTN_BLOCK, 512), lambda b, i: (b * nb + i, kv_blk)),
            pl.BlockSpec((ATTN_BLOCK, 512), lambda b, i: (b * nb + jnp.maximum(i - 1, 0), kv_blk)),
            pl.BlockSpec((2, N_Q_HEADS, ATTN_BLOCK, 2 * ATTN_BLOCK), lambda b, i: (0, 0, 0, 0)),
        ],
        out_specs=pl.BlockSpec((ATTN_BLOCK, D_MODEL), lambda b, i: (b * nb + i, 0)),
        out_shape=jax.ShapeDtypeStruct((t, D_MODEL), BF16),
        compiler_params=pltpu.CompilerParams(
            dimension_semantics=("arbitrary", "arbitrary"), vmem_limit_bytes=VMEM_LIMIT),
        name="swa_attn",
    )(sink, proj, proj, proj, bias2)


def _softplus(x):
    return jnp.maximum(x, 0.0) + jnp.log1p(jnp.exp(-jnp.abs(x)))


def _bcast_heads(v, lo):
    pieces = []
    for j in range(N_SSM_HEADS // 2):
        a = jnp.broadcast_to(v[:, 2 * j:2 * j + 1], (CHUNK, LANES))
        b = jnp.broadcast_to(v[:, 2 * j + 1:2 * j + 2], (CHUNK, LANES))
        pieces.append(jnp.where(lo, a, b))
    return jnp.concatenate(pieces, axis=1)


def _ssd_kernel(z_ref, xs_ref, bc_ref, dt_ref, cw_ref, cb_ref, dtb_ref, a_ref, dsk_ref, ng_ref,
                y_ref, ext_scr, state_scr):
    @pl.when(pl.program_id(1) == 0)
    def _():
        ext_scr[0:CONV_CARRY, :] = jnp.zeros((CONV_CARRY, ext_scr.shape[1]), F32)
        state_scr[...] = jnp.zeros_like(state_scr)

    xbc = jnp.concatenate([xs_ref[...], bc_ref[...]], axis=1).astype(F32)
    ext_scr[CONV_CARRY:CONV_CARRY + CHUNK, :] = xbc
    acc = cb_ref[...] + cw_ref[3:4, :] * xbc
    for j in range(CONV_WIDTH - 1):
        off = CONV_CARRY - (CONV_WIDTH - 1) + j
        acc = acc + cw_ref[j:j + 1, :] * ext_scr[off:off + CHUNK, :]
    ext_scr[0:CONV_CARRY, :] = xbc[CHUNK - CONV_CARRY:, :]
    act = acc * jax.nn.sigmoid(acc)
    xs_c = act[:, :D_INNER]
    bm = act[:, D_INNER:D_INNER + N_SSM_GROUPS * D_STATE].astype(BF16)
    cm = act[:, D_INNER + N_SSM_GROUPS * D_STATE:].astype(BF16)

    dt = _softplus(dt_ref[...] + dtb_ref[...])
    da = dt * a_ref[...]
    row = lax.broadcasted_iota(I32, (CHUNK, CHUNK), 0)
    col = lax.broadcasted_iota(I32, (CHUNK, CHUNK), 1)
    causal = row >= col
    tril = jnp.where(causal, 1.0, 0.0).astype(BF16)
    hi = da.astype(BF16)
    r1 = da - hi.astype(F32)
    mid = r1.astype(BF16)
    low = (r1 - mid.astype(F32)).astype(BF16)
    acum = (jnp.dot(tril, hi, preferred_element_type=F32) + jnp.dot(tril, mid, preferred_element_type=F32)
            + jnp.dot(tril, low, preferred_element_type=F32))
    acum_t = acum.T
    exp_a = jnp.exp(acum)
    to_end = jnp.exp(acum[CHUNK - 1:CHUNK, :] - acum)

    lo = lax.broadcasted_iota(I32, (CHUNK, LANES), 1) < (LANES // 2)
    dt_b = _bcast_heads(dt, lo)
    exp_a_b = _bcast_heads(exp_a, lo)
    xdt = xs_c * dt_b
    xw_bf = (xdt * _bcast_heads(to_end, lo)).astype(BF16)

    state_bf = state_scr[...].astype(BF16)
    gw = D_INNER // N_SSM_GROUPS
    y_parts = []
    for g in range(N_SSM_GROUPS):
        c_g = cm[:, g * D_STATE:(g + 1) * D_STATE]
        b_g = bm[:, g * D_STATE:(g + 1) * D_STATE]
        cb = lax.dot_general(c_g, b_g, (((1,), (1,)), ((), ())), preferred_element_type=F32)
        y_off = jnp.dot(c_g, state_bf[:, g * gw:(g + 1) * gw], preferred_element_type=F32)
        yd = []
        for jj in range(4):
            j = g * 4 + jj
            ms = []
            for hh in range(2):
                e = 2 * j + hh
                seg = (jnp.broadcast_to(acum[:, e:e + 1], (CHUNK, CHUNK))
                       - jnp.broadcast_to(acum_t[e:e + 1, :], (CHUNK, CHUNK)))
                dec = jnp.exp(jnp.where(causal, seg, -jnp.inf))
                ms.append((cb * dec).astype(BF16))
            lhs = jnp.concatenate(ms, axis=1)
            xp = xdt[:, j * LANES:(j + 1) * LANES]
            rhs = jnp.concatenate([jnp.where(lo, xp, 0.0), jnp.where(lo, 0.0, xp)], axis=0).astype(BF16)
            yd.append(jnp.dot(lhs, rhs, preferred_element_type=F32))
        y_diag = jnp.concatenate(yd, axis=1)
        y_parts.append(y_diag + y_off * exp_a_b[:, g * gw:(g + 1) * gw])
        b_t = act[:, D_INNER + g * D_STATE:D_INNER + (g + 1) * D_STATE].T.astype(BF16)
        upd = jnp.dot(b_t, xw_bf[:, g * gw:(g + 1) * gw], preferred_element_type=F32)
        state_scr[:, g * gw:(g + 1) * gw] = (
            state_scr[:, g * gw:(g + 1) * gw] * exp_a_b[CHUNK - 1:CHUNK, g * gw:(g + 1) * gw] + upd)
    y = jnp.concatenate(y_parts, axis=1) + dsk_ref[...] * xs_c

    zf = z_ref[...].astype(F32)
    hg = y * (zf * jax.nn.sigmoid(zf))
    outs = []
    for g in range(N_SSM_GROUPS):
        h_g = hg[:, g * gw:(g + 1) * gw]
        ms_g = jnp.mean(h_g * h_g, axis=-1, keepdims=True)
        outs.append(h_g * lax.rsqrt(ms_g + LN_EPS))
    y_ref[...] = (jnp.concatenate(outs, axis=1) * ng_ref[...]).astype(BF16)


def _ssd(proj, dt_raw, conv_w, conv_b, dt_bias, a_neg, dskip_b, norm_g, nbatch):
    t = proj.shape[0]
    nc = SEQ // CHUNK
    conv_ch = D_INNER + 2 * N_SSM_GROUPS * D_STATE
    const = lambda b, c: (0, 0)
    return pl.pallas_call(
        _ssd_kernel,
        grid=(nbatch, nc),
        in_specs=[
            pl.BlockSpec((CHUNK, D_INNER), lambda b, c: (b * nc + c, COL_Z // D_INNER)),
            pl.BlockSpec((CHUNK, D_INNER), lambda b, c: (b * nc + c, COL_XS // D_INNER)),
            pl.BlockSpec((CHUNK, 1024), lambda b, c: (b * nc + c, COL_B // 1024)),
            pl.BlockSpec((CHUNK, LANES), lambda b, c: (b * nc + c, 0)),
            pl.BlockSpec((CONV_WIDTH, conv_ch), const),
            pl.BlockSpec((1, conv_ch), const),
            pl.BlockSpec((1, LANES), const),
            pl.BlockSpec((1, LANES), const),
            pl.BlockSpec((1, D_INNER), const),
            pl.BlockSpec((1, D_INNER), const),
        ],
        out_specs=pl.BlockSpec((CHUNK, D_INNER), lambda b, c: (b * nc + c, 0)),
        out_shape=jax.ShapeDtypeStruct((t, D_INNER), BF16),
        scratch_shapes=[
            pltpu.VMEM((CONV_CARRY + CHUNK, conv_ch), F32),
            pltpu.VMEM((D_STATE, D_INNER), F32),
        ],
        compiler_params=pltpu.CompilerParams(
            dimension_semantics=("arbitrary", "arbitrary"), vmem_limit_bytes=VMEM_LIMIT),
        name="ssd",
    )(proj, proj, proj, dt_raw, conv_w, conv_b, dt_bias, a_neg, dskip_b, norm_g)


def _mix_route_kernel(attn_ref, y_ref, gate_ref, x_ref, lng_ref, lnb_ref, bg_ref, wao_ref, wso_ref, wo_ref,
                      l1g_ref, l1b_ref, wr_ref, h1_ref, hp_ref, eid_ref, gw_ref):
    ab = jnp.dot(attn_ref[...], wao_ref[...], preferred_element_type=F32)
    sb = jnp.dot(y_ref[...], wso_ref[...], preferred_element_type=F32)
    gate = jax.nn.sigmoid(gate_ref[...].astype(F32) + bg_ref[...])
    mixin = (gate[:, :D_MODEL] * ab + gate[:, D_MODEL:] * sb).astype(BF16)
    mix = jnp.dot(mixin, wo_ref[...], preferred_element_type=F32)
    h = _layer_norm(x_ref[...], lng_ref[...], lnb_ref[...])
    h1 = _layer_norm(DEEPNORM_ALPHA * h + mix, l1g_ref[...], l1b_ref[...])
    h1_ref[...] = h1

    h1b = h1.astype(BF16)
    half = D_MODEL // 2
    wa = lax.bitcast_convert_type(h1b[:, :half].astype(F32), U32)
    wb = lax.bitcast_convert_type(h1b[:, half:].astype(F32), U32)
    hp_ref[...] = lax.shift_right_logical(wa, jnp.uint32(16)) | (wb & jnp.uint32(0xFFFF0000))

    logits = jnp.dot(h1b, wr_ref[...], preferred_element_type=F32)
    tm = logits.shape[0]
    lane = lax.broadcasted_iota(I32, (tm, LANES), 1)
    neg = -jnp.inf
    lg = jnp.where(lane < N_EXPERT_GROUPS, logits, neg)
    mg = jnp.max(lg, axis=-1, keepdims=True)
    grp = jnp.min(jnp.where(lg == mg, lane, LANES), axis=-1, keepdims=True)
    p_grp = 1.0 / jnp.sum(jnp.exp(lg - mg), axis=-1, keepdims=True)
    el = lane - N_EXPERT_GROUPS
    in_grp = (el >= 0) & (el < N_EXPERTS) & (lax.shift_right_arithmetic(el, 3) == grp)
    le = jnp.where(in_grp, logits, neg)
    m1 = jnp.max(le, axis=-1, keepdims=True)
    i1 = jnp.min(jnp.where(le == m1, lane, LANES), axis=-1, keepdims=True)
    le2 = jnp.where(lane == i1, neg, le)
    m2 = jnp.max(le2, axis=-1, keepdims=True)
    i2 = jnp.min(jnp.where(le2 == m2, lane, LANES), axis=-1, keepdims=True)
    e2 = jnp.exp(m2 - m1)
    p1 = p_grp / (1.0 + e2)
    p2 = p1 * e2
    eid_ref[...] = jnp.where(lane == 0, i1 - N_EXPERT_GROUPS, jnp.where(lane == 1, i2 - N_EXPERT_GROUPS, 0))
    gw_ref[...] = jnp.where(lane == 0, p1, jnp.where(lane == 1, p2, 0.0))


def _mix_route(attn, yssm, proj, x2, lng, lnb, b_gate, w_ao, w_so, w_o, l1g, l1b, w_r):
    t = x2.shape[0]
    const = lambda i: (0, 0)
    row = lambda i: (i, 0)
    return pl.pallas_call(
        _mix_route_kernel,
        grid=(t // TM_MIX,),
        in_specs=[
            pl.BlockSpec((TM_MIX, D_MODEL), row),
            pl.BlockSpec((TM_MIX, D_INNER), row),
            pl.BlockSpec((TM_MIX, 2 * D_MODEL), lambda i: (i, COL_GA // (2 * D_MODEL))),
            pl.BlockSpec((TM_MIX, D_MODEL), row),
            pl.BlockSpec((1, D_MODEL), const),
            pl.BlockSpec((1, D_MODEL), const),
            pl.BlockSpec((1, 2 * D_MODEL), const),
            pl.BlockSpec((D_MODEL, D_MODEL), const),
            pl.BlockSpec((D_INNER, D_MODEL), const),
            pl.BlockSpec((D_MODEL, D_MODEL), const),
            pl.BlockSpec((1, D_MODEL), const),
            pl.BlockSpec((1, D_MODEL), const),
            pl.BlockSpec((D_MODEL, LANES), const),
        ],
        out_specs=[
            pl.BlockSpec((TM_MIX, D_MODEL), row),
            pl.BlockSpec((TM_MIX, D_MODEL // 2), row),
            pl.BlockSpec((TM_MIX, LANES), row),
            pl.BlockSpec((TM_MIX, LANES), row),
        ],
        out_shape=[
            jax.ShapeDtypeStruct((t, D_MODEL), F32),
            jax.ShapeDtypeStruct((t, D_MODEL // 2), U32),
            jax.ShapeDtypeStruct((t, LANES), I32),
            jax.ShapeDtypeStruct((t, LANES), F32),
        ],
        compiler_params=pltpu.CompilerParams(
            dimension_semantics=("arbitrary",), vmem_limit_bytes=VMEM_LIMIT),
        name="mix_route",
    )(attn, yssm, proj, x2, lng, lnb, b_gate, w_ao, w_so, w_o, l1g, l1b, w_r)


def _moe_kernel(be_ref, src0_ref, nxt_ref, dst_ref, hp_hbm, wgu_ref, wdn_ref, rw_ref, y_hbm,
                xbuf, ybuf, gsem, ssem):
    del be_ref
    i = pl.program_id(0)
    nblk = pl.num_programs(0)
    slot = lax.rem(i, 2)
    other = 1 - slot

    def issue_gather(idx_ref, s):
        for j in range(BM_MOE):
            tok = idx_ref[0, 0, j]
            pltpu.make_async_copy(hp_hbm.at[pl.ds(tok, 1), :], xbuf.at[s, pl.ds(j, 1), :], gsem.at[s]).start()

    def wait_gather(s):
        pltpu.make_async_copy(hp_hbm.at[pl.ds(0, BM_MOE), :], xbuf.at[s], gsem.at[s]).wait()

    def wait_scatter(s):
        pltpu.make_async_copy(ybuf.at[s], y_hbm.at[pl.ds(0, BM_MOE), :], ssem.at[s]).wait()

    @pl.when(i == 0)
    def _():
        issue_gather(src0_ref, 0)

    @pl.when(i >= 2)
    def _():
        wait_scatter(slot)

    wait_gather(slot)
    issue_gather(nxt_ref, other)

    u = xbuf[slot]
    xa = lax.bitcast_convert_type(lax.shift_left(u, jnp.uint32(16)), F32)
    xb = lax.bitcast_convert_type(u & jnp.uint32(0xFFFF0000), F32)
    x = jnp.concatenate([xa, xb], axis=1).astype(BF16)
    gu = jnp.dot(x, wgu_ref[0], preferred_element_type=F32)
    gt = gu[:, :D_EXPERT]
    hid = (gt * jax.nn.sigmoid(gt) * gu[:, D_EXPERT:]).astype(BF16)
    y = jnp.dot(hid, wdn_ref[0], preferred_element_type=F32)
    ybuf[slot] = y * rw_ref[...]
    for j in range(BM_MOE):
        d = dst_ref[0, 0, j]
        pltpu.make_async_copy(ybuf.at[slot, pl.ds(j, 1), :], y_hbm.at[pl.ds(d, 1), :], ssem.at[slot]).start()

    @pl.when(i == nblk - 1)
    def _():
        wait_gather(other)
        wait_scatter(slot)

        @pl.when(nblk >= 2)
        def _():
            wait_scatter(other)


def _moe(block_e, row_src, row_dst, hp, w_gu, w_dn, row_w, n_y_rows):
    n_rows = row_src.shape[0]
    nblk = n_rows // BM_MOE
    src3 = row_src.reshape(nblk, 1, BM_MOE)
    nxt3 = jnp.concatenate([src3[1:], src3[-1:]], axis=0)
    dst3 = row_dst.reshape(nblk, 1, BM_MOE)
    idx_spec = pl.BlockSpec((1, 1, BM_MOE), lambda i, be: (i, 0, 0), memory_space=pltpu.SMEM)
    grid_spec = pltpu.PrefetchScalarGridSpec(
        num_scalar_prefetch=1,
        grid=(nblk,),
        in_specs=[
            pl.BlockSpec((1, 1, BM_MOE), lambda i, be: (0, 0, 0), memory_space=pltpu.SMEM),
            idx_spec,
            idx_spec,
            pl.BlockSpec(memory_space=pl.ANY),
            pl.BlockSpec((1, D_MODEL, 2 * D_EXPERT), lambda i, be: (be[i], 0, 0)),
            pl.BlockSpec((1, D_EXPERT, D_MODEL), lambda i, be: (be[i], 0, 0)),
            pl.BlockSpec((BM_MOE, 1), lambda i, be: (i, 0)),
        ],
        out_specs=pl.BlockSpec(memory_space=pl.ANY),
        scratch_shapes=[
            pltpu.VMEM((2, BM_MOE, D_MODEL // 2), U32),
            pltpu.VMEM((2, BM_MOE, D_MODEL), F32),
            pltpu.SemaphoreType.DMA((2,)),
            pltpu.SemaphoreType.DMA((2,)),
        ],
    )
    return pl.pallas_call(
        _moe_kernel,
        grid_spec=grid_spec,
        out_shape=jax.ShapeDtypeStruct((n_y_rows, D_MODEL), F32),
        compiler_params=pltpu.CompilerParams(
            dimension_semantics=("arbitrary",), vmem_limit_bytes=VMEM_LIMIT),
        name="moe",
    )(block_e, src3, nxt3, dst3, hp, w_gu, w_dn, row_w)


def _final_kernel(h1_ref, y0_ref, y1_ref, g_ref, b_ref, o_ref):
    o_ref[...] = _layer_norm(DEEPNORM_ALPHA * h1_ref[...] + (y0_ref[...] + y1_ref[...]), g_ref[...], b_ref[...])


def _final_ln(h1, y, g, b):
    t = h1.shape[0]
    nt = t // TM_FIN
    const = lambda i: (0, 0)
    return pl.pallas_call(
        _final_kernel,
        grid=(nt,),
        in_specs=[
            pl.BlockSpec((TM_FIN, D_MODEL), lambda i: (i, 0)),
            pl.BlockSpec((TM_FIN, D_MODEL), lambda i: (i, 0)),
            pl.BlockSpec((TM_FIN, D_MODEL), lambda i: (i + nt, 0)),
            pl.BlockSpec((1, D_MODEL), const),
            pl.BlockSpec((1, D_MODEL), const),
        ],
        out_specs=pl.BlockSpec((TM_FIN, D_MODEL), lambda i: (i, 0)),
        out_shape=jax.ShapeDtypeStruct((t, D_MODEL), F32),
        compiler_params=pltpu.CompilerParams(
            dimension_semantics=("arbitrary",), vmem_limit_bytes=VMEM_LIMIT),
        name="final_ln",
    )(h1, y, y, g, b)


def _t5_causal_bucket(dist):
    max_exact = NUM_BUCKETS // 2
    d_f = jnp.maximum(dist, 1).astype(F32)
    large = max_exact + (jnp.log(d_f / max_exact) / math.log(MAX_DISTANCE / max_exact)
                         * (NUM_BUCKETS - max_exact)).astype(I32)
    large = jnp.minimum(large, NUM_BUCKETS - 1)
    return jnp.where(dist < max_exact, dist, large)


def _attn_bias_tables(rel_bias):
    qi = jnp.arange(ATTN_BLOCK)[:, None]
    kj = jnp.arange(2 * ATTN_BLOCK)[None, :]
    dist = qi + ATTN_BLOCK - kj
    bias = rel_bias[_t5_causal_bucket(jnp.clip(dist, 0, None))].astype(F32)
    bias = jnp.transpose(bias, (2, 0, 1))
    in_window = (dist >= 0) & (dist < WINDOW)
    normal = jnp.where(in_window[None], bias, -jnp.inf)
    first = jnp.where((in_window & (kj >= ATTN_BLOCK))[None], bias, -jnp.inf)
    return jnp.stack([normal, first], axis=0)


def _route_metadata(eid, gw, t):
    n_assign = 2 * t
    n_blk = n_assign // BM_MOE + N_EXPERTS
    n_rows = n_blk * BM_MOE
    e = eid.reshape(-1)
    order = jnp.argsort(e).astype(I32)
    counts = jnp.sum((e[:, None] == jnp.arange(N_EXPERTS, dtype=I32)[None, :]).astype(I32), axis=0)
    starts = jnp.cumsum(counts) - counts
    padded = (counts + BM_MOE - 1) // BM_MOE * BM_MOE
    pends = jnp.cumsum(padded)
    pstarts = pends - padded
    block_e = jnp.minimum(jnp.searchsorted(pends, jnp.arange(n_blk, dtype=I32) * BM_MOE, side='right'),
                          N_EXPERTS - 1).astype(I32)
    r = jnp.arange(n_rows, dtype=I32)
    re = jnp.repeat(block_e, BM_MOE)
    idx = r - pstarts[re]
    valid = (idx >= 0) & (idx < counts[re])
    a = order[jnp.clip(starts[re] + idx, 0, n_assign - 1)]
    row_src = jnp.where(valid, a // 2, 0).astype(I32)
    dump = n_assign + (r % (2 * BM_MOE))
    row_dst = jnp.where(valid, (a % 2) * t + a // 2, dump).astype(I32)
    row_w = jnp.where(valid, gw.reshape(-1)[a], 0.0).astype(F32)
    return block_e, row_src, row_dst, row_w[:, None], n_assign + 2 * BM_MOE


def kernel(x, ln_in_g, ln_in_b, rel_bias, w_in, b_gate, attn_sink, conv_w, conv_b, dt_bias, a_log, d_skip,
           ssm_norm_g, w_attn_out, w_ssm_out, w_out, ln1_g, ln1_b, w_group_router, w_expert_router,
           w_gate_e, w_up_e, w_down_e, ln2_g, ln2_b):
    nbatch, seq, d = x.shape
    assert seq == SEQ and d == D_MODEL
    t = nbatch * seq
    x2 = x.reshape(t, d)
    l = 0

    w = w_in[l]
    q_w, k_w, v_w, z_w, xs_w, b_w, c_w, dt_w, ga_w, gs_w = jnp.split(
        w, [1024, 1152, 1280, 3328, 5376, 5888, 6400, 6432, 7456], axis=1)
    k0, k1 = k_w[:, :HEAD_DIM], k_w[:, HEAD_DIM:]
    v0, v1 = v_w[:, :HEAD_DIM], v_w[:, HEAD_DIM:]
    w_cat = jnp.concatenate([z_w, xs_w, b_w, c_w, q_w * (HEAD_DIM ** -0.5), ga_w, gs_w,
                             k0, k0, k1, k1, v0, v0, v1, v1], axis=1).astype(BF16)
    w_dt = jnp.pad(dt_w, ((0, 0), (0, LANES - N_SSM_HEADS))).astype(BF16)
    row = lambda v: v.reshape(1, -1).astype(F32)
    pad_heads = lambda v: jnp.pad(v.astype(F32), (0, LANES - N_SSM_HEADS)).reshape(1, LANES)
    a_neg = pad_heads(-jnp.exp(a_log[l].astype(F32)))
    dskip_b = jnp.repeat(d_skip[l].astype(F32), D_INNER // N_SSM_HEADS).reshape(1, D_INNER)
    w_r = jnp.pad(jnp.concatenate([w_group_router[l], w_expert_router[l]], axis=1),
                  ((0, 0), (0, LANES - N_EXPERT_GROUPS - N_EXPERTS))).astype(BF16)
    w_gu = jnp.concatenate([w_gate_e[l], w_up_e[l]], axis=2).astype(BF16)
    w_dn = w_down_e[l].astype(BF16)
    bias2 = _attn_bias_tables(rel_bias)

    proj, dt_raw = _ln_inproj(x2, row(ln_in_g), row(ln_in_b), w_cat, w_dt)
    attn = _swa_attn(proj, bias2, attn_sink[l].astype(F32), nbatch)
    yssm = _ssd(proj, dt_raw, conv_w[l].astype(F32), row(conv_b[l]), pad_heads(dt_bias[l]), a_neg,
                dskip_b, row(ssm_norm_g[l]), nbatch)
    h1, hp, eid, gw = _mix_route(attn, yssm, proj, x2, row(ln_in_g), row(ln_in_b), row(b_gate[l]),
                                 w_attn_out[l].astype(BF16), w_ssm_out[l].astype(BF16), w_out[l].astype(BF16),
                                 row(ln1_g[l]), row(ln1_b[l]), w_r)

    block_e, row_src, row_dst, row_w, n_y_rows = _route_metadata(eid[:, :2], gw[:, :2], t)
    y = _moe(block_e, row_src, row_dst, hp, w_gu, w_dn, row_w, n_y_rows)
    out = _final_ln(h1, y, row(ln2_g[l]), row(ln2_b[l]))
    return out.reshape(nbatch, seq, d)
```

```python
import functools
import math

import jax
import jax.numpy as jnp
from jax import lax
from jax.experimental import pallas as pl
from jax.experimental.pallas import tpu as pltpu

F32 = jnp.float32
BF16 = jnp.bfloat16
I32 = jnp.int32
U32 = jnp.uint32

D_MODEL = 1024
SEQ = 2048
HEAD_DIM = 64
N_Q_HEADS = 16
N_KV_HEADS = 2
ATTN_BLOCK = 128
WINDOW = 128
NUM_BUCKETS = 32
MAX_DISTANCE = 128
D_INNER = 2048
N_SSM_HEADS = 32
N_SSM_GROUPS = 4
D_STATE = 128
CONV_WIDTH = 4
CHUNK = 128
N_EXPERT_GROUPS = 4
EXPERTS_PER_GROUP = 8
N_EXPERTS = 32
D_EXPERT = 512
DEEPNORM_ALPHA = 2.0 ** 0.25
LN_EPS = 1e-5

LANES = 128
SUBLANES = 8

COL_Z = 0
COL_XS = 2048
COL_B = 4096
COL_C = 4608
COL_Q = 5120
COL_GA = 6144
COL_GS = 7168
COL_KV = 8192
N_PROJ = 8704

TM_PROJ = 1024
TN_PROJ = 2176
TM_MIX = 256
TM_FIN = 512
BM_MOE = 256
HALF_MOE = BM_MOE // 2
CONV_CARRY = SUBLANES

VMEM_LIMIT = 56 * 1024 * 1024


def _layer_norm(x, g, b):
    mu = jnp.mean(x, axis=-1, keepdims=True)
    xc = x - mu
    var = jnp.mean(xc * xc, axis=-1, keepdims=True)
    return xc * lax.rsqrt(var + LN_EPS) * g + b


assert D_MODEL == SUBLANES * LANES


def _store_token_tiles(ref, val, n):
    for c in range(SUBLANES):
        ref[pl.ds(c, n, stride=SUBLANES), :] = val[:, c * LANES:(c + 1) * LANES]


def _load_token_tiles(ref, n):
    return jnp.concatenate([ref[pl.ds(c, n, stride=SUBLANES), :] for c in range(SUBLANES)], axis=1)


def _ln_inproj_kernel(x_ref, g_ref, b_ref, w_ref, wdt_ref, o_ref, dt_ref, h_scr):
    @pl.when(pl.program_id(1) == 0)
    def _():
        hb = _layer_norm(x_ref[...], g_ref[...], b_ref[...]).astype(BF16)
        h_scr[...] = hb
        dt_ref[...] = jnp.dot(hb, wdt_ref[...], preferred_element_type=F32)

    o_ref[...] = jnp.dot(h_scr[...], w_ref[...], preferred_element_type=F32).astype(BF16)


def _ln_inproj(x2, g, b, w_cat, w_dt):
    t = x2.shape[0]
    return pl.pallas_call(
        _ln_inproj_kernel,
        grid=(t // TM_PROJ, N_PROJ // TN_PROJ),
        in_specs=[
            pl.BlockSpec((TM_PROJ, D_MODEL), lambda i, j: (i, 0)),
            pl.BlockSpec((1, D_MODEL), lambda i, j: (0, 0)),
            pl.BlockSpec((1, D_MODEL), lambda i, j: (0, 0)),
            pl.BlockSpec((D_MODEL, TN_PROJ), lambda i, j: (0, j)),
            pl.BlockSpec((D_MODEL, LANES), lambda i, j: (0, 0)),
        ],
        out_specs=[
            pl.BlockSpec((TM_PROJ, TN_PROJ), lambda i, j: (i, j)),
            pl.BlockSpec((TM_PROJ, LANES), lambda i, j: (i, 0)),
        ],
        out_shape=[
            jax.ShapeDtypeStruct((t, N_PROJ), BF16),
            jax.ShapeDtypeStruct((t, LANES), F32),
        ],
        scratch_shapes=[pltpu.VMEM((TM_PROJ, D_MODEL), BF16)],
        compiler_params=pltpu.CompilerParams(
            dimension_semantics=("arbitrary", "arbitrary"), vmem_limit_bytes=VMEM_LIMIT),
        name="ln_inproj",
    )(x2, g, b, w_cat, w_dt)


def _attn_kernel(sink_ref, q_ref, kvc_ref, kvp_ref, bias_ref, o_ref):
    first = jnp.where(pl.program_id(1) == 0, 1, 0)
    lo = lax.broadcasted_iota(I32, (ATTN_BLOCK, LANES), 1) < HEAD_DIM
    for kv in range(N_KV_HEADS):
        ks = slice(kv * LANES, (kv + 1) * LANES)
        vs = slice((N_KV_HEADS + kv) * LANES, (N_KV_HEADS + kv + 1) * LANES)
        kcat = jnp.concatenate([kvp_ref[:, ks], kvc_ref[:, ks]], axis=0)
        vcat = jnp.concatenate([kvp_ref[:, vs], kvc_ref[:, vs]], axis=0)
        for pp in range(4):
            p = kv * 4 + pp
            qp = q_ref[:, p * LANES:(p + 1) * LANES].astype(F32)
            qm = jnp.concatenate([jnp.where(lo, qp, 0.0), jnp.where(lo, 0.0, qp)], axis=0).astype(BF16)
            s = lax.dot_general(qm, kcat, (((1,), (1,)), ((), ())), preferred_element_type=F32)
            outs = []
            for hh in range(2):
                h = 2 * p + hh
                sh = s[hh * ATTN_BLOCK:(hh + 1) * ATTN_BLOCK] + bias_ref[first, h]
                snk = sink_ref[h]
                m = jnp.maximum(jnp.max(sh, axis=-1, keepdims=True), snk)
                pe = jnp.exp(sh - m)
                den = jnp.sum(pe, axis=-1, keepdims=True) + jnp.exp(snk - m)
                o = jnp.dot(pe.astype(BF16), vcat, preferred_element_type=F32)
                outs.append(o / den)
            o_ref[:, p * LANES:(p + 1) * LANES] = jnp.where(lo, outs[0], outs[1]).astype(BF16)


def _swa_attn(proj, bias2, sink, nbatch):
    t = proj.shape[0]
    nb = SEQ // ATTN_BLOCK
    q_blk = COL_Q // D_MODEL
    kv_blk = COL_KV // 512
    return pl.pallas_call(
        _attn_kernel,
        grid=(nbatch, nb),
        in_specs=[
            pl.BlockSpec(memory_space=pltpu.SMEM),
            pl.BlockSpec((ATTN_BLOCK, D_MODEL), lambda b, i: (b * nb + i, q_blk)),
            pl.BlockSpec((ATTN_BLOCK, 512), lambda b, i: (b * nb + i, kv_blk)),
            pl.BlockSpec((ATTN_BLOCK, 512), lambda b, i: (b * nb + jnp.maximum(i - 1, 0), kv_blk)),
            pl.BlockSpec((2, N_Q_HEADS, ATTN_BLOCK, 2 * ATTN_BLOCK), lambda b, i: (0, 0, 0, 0)),
        ],
        out_specs=pl.BlockSpec((ATTN_BLOCK, D_MODEL), lambda b, i: (b * nb + i, 0)),
        out_shape=jax.ShapeDtypeStruct((t, D_MODEL), BF16),
        compiler_params=pltpu.CompilerParams(
            dimension_semantics=("arbitrary", "arbitrary"), vmem_limit_bytes=VMEM_LIMIT),
        name="swa_attn",
    )(sink, proj, proj, proj, bias2)


def _softplus(x):
    return jnp.maximum(x, 0.0) + jnp.log1p(jnp.exp(-jnp.abs(x)))


def _bcast_heads(v, lo):
    pieces = []
    for j in range(N_SSM_HEADS // 2):
        a = jnp.broadcast_to(v[:, 2 * j:2 * j + 1], (CHUNK, LANES))
        b = jnp.broadcast_to(v[:, 2 * j + 1:2 * j + 2], (CHUNK, LANES))
        pieces.append(jnp.where(lo, a, b))
    return jnp.concatenate(pieces, axis=1)


def _ssd_kernel(z_ref, xs_ref, bc_ref, dt_ref, cw_ref, cb_ref, dtb_ref, a_ref, dsk_ref, ng_ref,
                y_ref, ext_scr, state_scr):
    @pl.when(pl.program_id(1) == 0)
    def _():
        ext_scr[0:CONV_CARRY, :] = jnp.zeros((CONV_CARRY, ext_scr.shape[1]), F32)
        state_scr[...] = jnp.zeros_like(state_scr)

    xbc = jnp.concatenate([xs_ref[...], bc_ref[...]], axis=1).astype(F32)
    ext_scr[CONV_CARRY:CONV_CARRY + CHUNK, :] = xbc
    acc = cb_ref[...] + cw_ref[3:4, :] * xbc
    for j in range(CONV_WIDTH - 1):
        off = CONV_CARRY - (CONV_WIDTH - 1) + j
        acc = acc + cw_ref[j:j + 1, :] * ext_scr[off:off + CHUNK, :]
    ext_scr[0:CONV_CARRY, :] = xbc[CHUNK - CONV_CARRY:, :]
    act = acc * jax.nn.sigmoid(acc)
    xs_c = act[:, :D_INNER]
    bm = act[:, D_INNER:D_INNER + N_SSM_GROUPS * D_STATE].astype(BF16)
    cm = act[:, D_INNER + N_SSM_GROUPS * D_STATE:].astype(BF16)

    dt = _softplus(dt_ref[...] + dtb_ref[...])
    da = dt * a_ref[...]
    row = lax.broadcasted_iota(I32, (CHUNK, CHUNK), 0)
    col = lax.broadcasted_iota(I32, (CHUNK, CHUNK), 1)
    causal = row >= col
    tril = jnp.where(causal, 1.0, 0.0).astype(BF16)
    hi = da.astype(BF16)
    r1 = da - hi.astype(F32)
    mid = r1.astype(BF16)
    low = (r1 - mid.astype(F32)).astype(BF16)
    acum = (jnp.dot(tril, hi, preferred_element_type=F32) + jnp.dot(tril, mid, preferred_element_type=F32)
            + jnp.dot(tril, low, preferred_element_type=F32))
    acum_t = acum.T
    exp_a = jnp.exp(acum)
    to_end = jnp.exp(acum[CHUNK - 1:CHUNK, :] - acum)

    lo = lax.broadcasted_iota(I32, (CHUNK, LANES), 1) < (LANES // 2)
    dt_b = _bcast_heads(dt, lo)
    exp_a_b = _bcast_heads(exp_a, lo)
    xdt = xs_c * dt_b
    xw_bf = (xdt * _bcast_heads(to_end, lo)).astype(BF16)

    state_bf = state_scr[...].astype(BF16)
    gw = D_INNER // N_SSM_GROUPS
    y_parts = []
    for g in range(N_SSM_GROUPS):
        c_g = cm[:, g * D_STATE:(g + 1) * D_STATE]
        b_g = bm[:, g * D_STATE:(g + 1) * D_STATE]
        cb = lax.dot_general(c_g, b_g, (((1,), (1,)), ((), ())), preferred_element_type=F32)
        y_off = jnp.dot(c_g, state_bf[:, g * gw:(g + 1) * gw], preferred_element_type=F32)
        yd = []
        for jj in range(4):
            j = g * 4 + jj
            ms = []
            for hh in range(2):
                e = 2 * j + hh
                seg = (jnp.broadcast_to(acum[:, e:e + 1], (CHUNK, CHUNK))
                       - jnp.broadcast_to(acum_t[e:e + 1, :], (CHUNK, CHUNK)))
                dec = jnp.exp(jnp.where(causal, seg, -jnp.inf))
                ms.append((cb * dec).astype(BF16))
            lhs = jnp.concatenate(ms, axis=1)
            xp = xdt[:, j * LANES:(j + 1) * LANES]
            rhs = jnp.concatenate([jnp.where(lo, xp, 0.0), jnp.where(lo, 0.0, xp)], axis=0).astype(BF16)
            yd.append(jnp.dot(lhs, rhs, preferred_element_type=F32))
        y_diag = jnp.concatenate(yd, axis=1)
        y_parts.append(y_diag + y_off * exp_a_b[:, g * gw:(g + 1) * gw])
        b_t = act[:, D_INNER + g * D_STATE:D_INNER + (g + 1) * D_STATE].T.astype(BF16)
        upd = jnp.dot(b_t, xw_bf[:, g * gw:(g + 1) * gw], preferred_element_type=F32)
        state_scr[:, g * gw:(g + 1) * gw] = (
            state_scr[:, g * gw:(g + 1) * gw] * exp_a_b[CHUNK - 1:CHUNK, g * gw:(g + 1) * gw] + upd)
    y = jnp.concatenate(y_parts, axis=1) + dsk_ref[...] * xs_c

    zf = z_ref[...].astype(F32)
    hg = y * (zf * jax.nn.sigmoid(zf))
    outs = []
    for g in range(N_SSM_GROUPS):
        h_g = hg[:, g * gw:(g + 1) * gw]
        ms_g = jnp.mean(h_g * h_g, axis=-1, keepdims=True)
        outs.append(h_g * lax.rsqrt(ms_g + LN_EPS))
    y_ref[...] = (jnp.concatenate(outs, axis=1) * ng_ref[...]).astype(BF16)


def _ssd(proj, dt_raw, conv_w, conv_b, dt_bias, a_neg, dskip_b, norm_g, nbatch):
    t = proj.shape[0]
    nc = SEQ // CHUNK
    conv_ch = D_INNER + 2 * N_SSM_GROUPS * D_STATE
    const = lambda b, c: (0, 0)
    return pl.pallas_call(
        _ssd_kernel,
        grid=(nbatch, nc),
        in_specs=[
            pl.BlockSpec((CHUNK, D_INNER), lambda b, c: (b * nc + c, COL_Z // D_INNER)),
            pl.BlockSpec((CHUNK, D_INNER), lambda b, c: (b * nc + c, COL_XS // D_INNER)),
            pl.BlockSpec((CHUNK, 1024), lambda b, c: (b * nc + c, COL_B // 1024)),
            pl.BlockSpec((CHUNK, LANES), lambda b, c: (b * nc + c, 0)),
            pl.BlockSpec((CONV_WIDTH, conv_ch), const),
            pl.BlockSpec((1, conv_ch), const),
            pl.BlockSpec((1, LANES), const),
            pl.BlockSpec((1, LANES), const),
            pl.BlockSpec((1, D_INNER), const),
            pl.BlockSpec((1, D_INNER), const),
        ],
        out_specs=pl.BlockSpec((CHUNK, D_INNER), lambda b, c: (b * nc + c, 0)),
        out_shape=jax.ShapeDtypeStruct((t, D_INNER), BF16),
        scratch_shapes=[
            pltpu.VMEM((CONV_CARRY + CHUNK, conv_ch), F32),
            pltpu.VMEM((D_STATE, D_INNER), F32),
        ],
        compiler_params=pltpu.CompilerParams(
            dimension_semantics=("arbitrary", "arbitrary"), vmem_limit_bytes=VMEM_LIMIT),
        name="ssd",
    )(proj, proj, proj, dt_raw, conv_w, conv_b, dt_bias, a_neg, dskip_b, norm_g)


def _mix_route_kernel(attn_ref, y_ref, gate_ref, x_ref, lng_ref, lnb_ref, bg_ref, wao_ref, wso_ref, wo_ref,
                      l1g_ref, l1b_ref, wr_ref, h1t_ref, eid_ref, gw_ref, cnt_ref):
    ab = jnp.dot(attn_ref[...], wao_ref[...], preferred_element_type=F32)
    sb = jnp.dot(y_ref[...], wso_ref[...], preferred_element_type=F32)
    gate = jax.nn.sigmoid(gate_ref[...].astype(F32) + bg_ref[...])
    mixin = (gate[:, :D_MODEL] * ab + gate[:, D_MODEL:] * sb).astype(BF16)
    mix = jnp.dot(mixin, wo_ref[...], preferred_element_type=F32)
    h = _layer_norm(x_ref[...], lng_ref[...], lnb_ref[...])
    h1 = _layer_norm(DEEPNORM_ALPHA * h + mix, l1g_ref[...], l1b_ref[...])
    tm = h1.shape[0]
    _store_token_tiles(h1t_ref, h1, tm)

    logits = jnp.dot(h1.astype(BF16), wr_ref[...], preferred_element_type=F32)
    lane = lax.broadcasted_iota(I32, (tm, LANES), 1)
    neg = -jnp.inf
    lg = jnp.where(lane < N_EXPERT_GROUPS, logits, neg)
    mg = jnp.max(lg, axis=-1, keepdims=True)
    grp = jnp.min(jnp.where(lg == mg, lane, LANES), axis=-1, keepdims=True)
    p_grp = 1.0 / jnp.sum(jnp.exp(lg - mg), axis=-1, keepdims=True)
    el = lane - N_EXPERT_GROUPS
    in_grp = (el >= 0) & (el < N_EXPERTS) & (lax.shift_right_arithmetic(el, 3) == grp)
    le = jnp.where(in_grp, logits, neg)
    m1 = jnp.max(le, axis=-1, keepdims=True)
    i1 = jnp.min(jnp.where(le == m1, lane, LANES), axis=-1, keepdims=True)
    le2 = jnp.where(lane == i1, neg, le)
    m2 = jnp.max(le2, axis=-1, keepdims=True)
    i2 = jnp.min(jnp.where(le2 == m2, lane, LANES), axis=-1, keepdims=True)
    e2 = jnp.exp(m2 - m1)
    p1 = p_grp / (1.0 + e2)
    p2 = p1 * e2
    eid_ref[...] = jnp.where(lane == 0, i1 - N_EXPERT_GROUPS, jnp.where(lane == 1, i2 - N_EXPERT_GROUPS, 0))
    gw_ref[...] = jnp.where(lane == 0, p1, jnp.where(lane == 1, p2, 0.0))

    @pl.when(pl.program_id(0) == 0)
    def _():
        cnt_ref[...] = jnp.zeros_like(cnt_ref)

    picked = jnp.where((lane == i1) | (lane == i2), 1, 0)
    cnt_ref[...] += jnp.sum(picked, axis=0, keepdims=True)


def _mix_route(attn, yssm, proj, x2, lng, lnb, b_gate, w_ao, w_so, w_o, l1g, l1b, w_r):
    t = x2.shape[0]
    const = lambda i: (0, 0)
    row = lambda i: (i, 0)
    return pl.pallas_call(
        _mix_route_kernel,
        grid=(t // TM_MIX,),
        in_specs=[
            pl.BlockSpec((TM_MIX, D_MODEL), row),
            pl.BlockSpec((TM_MIX, D_INNER), row),
            pl.BlockSpec((TM_MIX, 2 * D_MODEL), lambda i: (i, COL_GA // (2 * D_MODEL))),
            pl.BlockSpec((TM_MIX, D_MODEL), row),
            pl.BlockSpec((1, D_MODEL), const),
            pl.BlockSpec((1, D_MODEL), const),
            pl.BlockSpec((1, 2 * D_MODEL), const),
            pl.BlockSpec((D_MODEL, D_MODEL), const),
            pl.BlockSpec((D_INNER, D_MODEL), const),
            pl.BlockSpec((D_MODEL, D_MODEL), const),
            pl.BlockSpec((1, D_MODEL), const),
            pl.BlockSpec((1, D_MODEL), const),
            pl.BlockSpec((D_MODEL, LANES), const),
        ],
        out_specs=[
            pl.BlockSpec((TM_MIX * SUBLANES, LANES), row),
            pl.BlockSpec((TM_MIX, LANES), row),
            pl.BlockSpec((TM_MIX, LANES), row),
            pl.BlockSpec((1, LANES), const),
        ],
        out_shape=[
            jax.ShapeDtypeStruct((t * SUBLANES, LANES), F32),
            jax.ShapeDtypeStruct((t, LANES), I32),
            jax.ShapeDtypeStruct((t, LANES), F32),
            jax.ShapeDtypeStruct((1, LANES), I32),
        ],
        compiler_params=pltpu.CompilerParams(
            dimension_semantics=("arbitrary",), vmem_limit_bytes=VMEM_LIMIT),
        name="mix_route",
    )(attn, yssm, proj, x2, lng, lnb, b_gate, w_ao, w_so, w_o, l1g, l1b, w_r)


def _moe_kernel(be_ref, cur_ref, nxt_ref, dst_ref, h_hbm, wg_ref, wu_ref, wd_ref, y_hbm,
                xbuf, ybuf, wg_bf, wu_bf, wd_bf, gsem, ssem):
    i = pl.program_id(0)
    nblk = pl.num_programs(0)

    def tile(j):
        return pl.ds(j * SUBLANES, SUBLANES)

    def issue_gather(idx_ref, base, s):
        for j in range(HALF_MOE):
            r8 = pl.multiple_of(idx_ref[0, 0, base + j], SUBLANES)
            pltpu.make_async_copy(h_hbm.at[pl.ds(r8, SUBLANES), :], xbuf.at[s, tile(j), :], gsem.at[s]).start()

    def wait_gather(s):
        pltpu.make_async_copy(h_hbm.at[pl.ds(0, HALF_MOE * SUBLANES), :], xbuf.at[s], gsem.at[s]).wait()

    def issue_scatter(base, s):
        for j in range(HALF_MOE):
            r8 = pl.multiple_of(dst_ref[0, 0, base + j], SUBLANES)
            pltpu.make_async_copy(ybuf.at[s, tile(j), :], y_hbm.at[pl.ds(r8, SUBLANES), :], ssem.at[s]).start()

    def wait_scatter(s):
        pltpu.make_async_copy(ybuf.at[s], y_hbm.at[pl.ds(0, HALF_MOE * SUBLANES), :], ssem.at[s]).wait()

    def expert_half(s):
        x = _load_token_tiles(xbuf.at[s], HALF_MOE).astype(BF16)
        gt = jnp.dot(x, wg_bf[...], preferred_element_type=F32)
        up = jnp.dot(x, wu_bf[...], preferred_element_type=F32)
        hid = (gt * jax.nn.sigmoid(gt) * up).astype(BF16)
        y = jnp.dot(hid, wd_bf[...], preferred_element_type=F32)
        _store_token_tiles(ybuf.at[s], y, HALF_MOE)

    @pl.when(i == 0)
    def _():
        issue_gather(cur_ref, 0, 0)

    @pl.when((i == 0) | (be_ref[i] != be_ref[jnp.maximum(i - 1, 0)]))
    def _():
        wg_bf[...] = wg_ref[0].astype(BF16)
        wu_bf[...] = wu_ref[0].astype(BF16)
        wd_bf[...] = wd_ref[0].astype(BF16)

    @pl.when(i > 0)
    def _():
        wait_scatter(0)

    wait_gather(0)
    issue_gather(cur_ref, HALF_MOE, 1)
    expert_half(0)
    issue_scatter(0, 0)

    @pl.when(i > 0)
    def _():
        wait_scatter(1)

    wait_gather(1)
    issue_gather(nxt_ref, 0, 0)
    expert_half(1)
    issue_scatter(HALF_MOE, 1)

    @pl.when(i == nblk - 1)
    def _():
        wait_gather(0)
        wait_scatter(0)
        wait_scatter(1)


def _moe(block_e, src8, dst8, h1t, w_gate, w_up, w_down, n_y_tiles):
    nblk = src8.shape[0] // BM_MOE
    src3 = src8.reshape(nblk, 1, BM_MOE)
    nxt3 = jnp.concatenate([src3[1:], src3[-1:]], axis=0)
    dst3 = dst8.reshape(nblk, 1, BM_MOE)
    idx_spec = pl.BlockSpec((1, 1, BM_MOE), lambda i, be: (i, 0, 0), memory_space=pltpu.SMEM)
    grid_spec = pltpu.PrefetchScalarGridSpec(
        num_scalar_prefetch=1,
        grid=(nblk,),
        in_specs=[
            idx_spec,
            idx_spec,
            idx_spec,
            pl.BlockSpec(memory_space=pl.ANY),
            pl.BlockSpec((1, D_MODEL, D_EXPERT), lambda i, be: (be[i], 0, 0)),
            pl.BlockSpec((1, D_MODEL, D_EXPERT), lambda i, be: (be[i], 0, 0)),
            pl.BlockSpec((1, D_EXPERT, D_MODEL), lambda i, be: (be[i], 0, 0)),
        ],
        out_specs=pl.BlockSpec(memory_space=pl.ANY),
        scratch_shapes=[
            pltpu.VMEM((2, HALF_MOE * SUBLANES, LANES), F32),
            pltpu.VMEM((2, HALF_MOE * SUBLANES, LANES), F32),
            pltpu.VMEM((D_MODEL, D_EXPERT), BF16),
            pltpu.VMEM((D_MODEL, D_EXPERT), BF16),
            pltpu.VMEM((D_EXPERT, D_MODEL), BF16),
            pltpu.SemaphoreType.DMA((2,)),
            pltpu.SemaphoreType.DMA((2,)),
        ],
    )
    return pl.pallas_call(
        _moe_kernel,
        grid_spec=grid_spec,
        out_shape=jax.ShapeDtypeStruct((n_y_tiles * SUBLANES, LANES), F32),
        compiler_params=pltpu.CompilerParams(
            dimension_semantics=("arbitrary",), vmem_limit_bytes=VMEM_LIMIT),
        name="moe",
    )(block_e, src3, nxt3, dst3, h1t, w_gate, w_up, w_down)


def _final_kernel(h1t_ref, y0_ref, y1_ref, gw_ref, g_ref, b_ref, o_ref):
    n = o_ref.shape[0]
    gw = gw_ref[...]
    ffn = gw[:, 0:1] * _load_token_tiles(y0_ref, n) + gw[:, 1:2] * _load_token_tiles(y1_ref, n)
    o_ref[...] = _layer_norm(DEEPNORM_ALPHA * _load_token_tiles(h1t_ref, n) + ffn, g_ref[...], b_ref[...])


def _final_ln(h1t, y, gw, g, b):
    t = gw.shape[0]
    nt = t // TM_FIN
    const = lambda i: (0, 0)
    tiles = (TM_FIN * SUBLANES, LANES)
    return pl.pallas_call(
        _final_kernel,
        grid=(nt,),
        in_specs=[
            pl.BlockSpec(tiles, lambda i: (i, 0)),
            pl.BlockSpec(tiles, lambda i: (i, 0)),
            pl.BlockSpec(tiles, lambda i: (i + nt, 0)),
            pl.BlockSpec((TM_FIN, LANES), lambda i: (i, 0)),
            pl.BlockSpec((1, D_MODEL), const),
            pl.BlockSpec((1, D_MODEL), const),
        ],
        out_specs=pl.BlockSpec((TM_FIN, D_MODEL), lambda i: (i, 0)),
        out_shape=jax.ShapeDtypeStruct((t, D_MODEL), F32),
        compiler_params=pltpu.CompilerParams(
            dimension_semantics=("arbitrary",), vmem_limit_bytes=VMEM_LIMIT),
        name="final_ln",
    )(h1t, y, y, gw, g, b)


def _t5_causal_bucket(dist):
    max_exact = NUM_BUCKETS // 2
    d_f = jnp.maximum(dist, 1).astype(F32)
    large = max_exact + (jnp.log(d_f / max_exact) / math.log(MAX_DISTANCE / max_exact)
                         * (NUM_BUCKETS - max_exact)).astype(I32)
    large = jnp.minimum(large, NUM_BUCKETS - 1)
    return jnp.where(dist < max_exact, dist, large)


def _attn_bias_tables(rel_bias):
    qi = jnp.arange(ATTN_BLOCK)[:, None]
    kj = jnp.arange(2 * ATTN_BLOCK)[None, :]
    dist = qi + ATTN_BLOCK - kj
    bias = rel_bias[_t5_causal_bucket(jnp.clip(dist, 0, None))].astype(F32)
    bias = jnp.transpose(bias, (2, 0, 1))
    in_window = (dist >= 0) & (dist < WINDOW)
    normal = jnp.where(in_window[None], bias, -jnp.inf)
    first = jnp.where((in_window & (kj >= ATTN_BLOCK))[None], bias, -jnp.inf)
    return jnp.stack([normal, first], axis=0)


ASSIGN_BITS = 17


def _route_metadata(eid, counts, t):
    n_assign = 2 * t
    assert n_assign <= 1 << ASSIGN_BITS
    n_blk = n_assign // BM_MOE + N_EXPERTS
    padded = (counts + BM_MOE - 1) // BM_MOE * BM_MOE
    pends = jnp.cumsum(padded)
    n_pad = padded - counts
    key = (eid.reshape(-1) << (ASSIGN_BITS + 1)) | jnp.arange(n_assign, dtype=I32)
    j = jnp.arange(BM_MOE, dtype=I32)[None, :]
    ee = jnp.arange(N_EXPERTS, dtype=I32)[:, None]
    pad_key = jnp.where(j < n_pad[:, None], (ee << (ASSIGN_BITS + 1)) | (1 << ASSIGN_BITS) | j,
                        jnp.iinfo(jnp.int32).max)
    keys = jnp.sort(jnp.concatenate([key, pad_key.reshape(-1)]))
    real = (keys & (1 << ASSIGN_BITS)) == 0
    a = keys & ((1 << ASSIGN_BITS) - 1)
    tok = a >> 1
    r = jnp.arange(n_blk * BM_MOE, dtype=I32)
    src8 = jnp.where(real, tok, 0) * SUBLANES
    dst8 = jnp.where(real, (a & 1) * t + tok, n_assign + (r % BM_MOE)) * SUBLANES
    blk_start = jnp.arange(n_blk, dtype=I32)[:, None] * BM_MOE
    block_e = jnp.minimum(jnp.sum((pends[None, :] <= blk_start).astype(I32), axis=1), N_EXPERTS - 1)
    return block_e.astype(I32), src8.astype(I32), dst8.astype(I32), n_assign + BM_MOE


def kernel(x, ln_in_g, ln_in_b, rel_bias, w_in, b_gate, attn_sink, conv_w, conv_b, dt_bias, a_log, d_skip,
           ssm_norm_g, w_attn_out, w_ssm_out, w_out, ln1_g, ln1_b, w_group_router, w_expert_router,
           w_gate_e, w_up_e, w_down_e, ln2_g, ln2_b):
    nbatch, seq, d = x.shape
    assert seq == SEQ and d == D_MODEL
    t = nbatch * seq
    x2 = x.reshape(t, d)
    l = 0

    w = w_in[l]
    q_w, k_w, v_w, z_w, xs_w, b_w, c_w, dt_w, ga_w, gs_w = jnp.split(
        w, [1024, 1152, 1280, 3328, 5376, 5888, 6400, 6432, 7456], axis=1)
    k0, k1 = k_w[:, :HEAD_DIM], k_w[:, HEAD_DIM:]
    v0, v1 = v_w[:, :HEAD_DIM], v_w[:, HEAD_DIM:]
    w_cat = jnp.concatenate([z_w, xs_w, b_w, c_w, q_w * (HEAD_DIM ** -0.5), ga_w, gs_w,
                             k0, k0, k1, k1, v0, v0, v1, v1], axis=1).astype(BF16)
    w_dt = jnp.pad(dt_w, ((0, 0), (0, LANES - N_SSM_HEADS))).astype(BF16)
    row = lambda v: v.reshape(1, -1).astype(F32)
    pad_heads = lambda v: jnp.pad(v.astype(F32), (0, LANES - N_SSM_HEADS)).reshape(1, LANES)
    a_neg = pad_heads(-jnp.exp(a_log[l].astype(F32)))
    dskip_b = jnp.repeat(d_skip[l].astype(F32), D_INNER // N_SSM_HEADS).reshape(1, D_INNER)
    w_r = jnp.pad(jnp.concatenate([w_group_router[l], w_expert_router[l]], axis=1),
                  ((0, 0), (0, LANES - N_EXPERT_GROUPS - N_EXPERTS))).astype(BF16)
    bias2 = _attn_bias_tables(rel_bias)

    proj, dt_raw = _ln_inproj(x2, row(ln_in_g), row(ln_in_b), w_cat, w_dt)
    attn = _swa_attn(proj, bias2, attn_sink[l].astype(F32), nbatch)
    yssm = _ssd(proj, dt_raw, conv_w[l].astype(F32), row(conv_b[l]), pad_heads(dt_bias[l]), a_neg,
                dskip_b, row(ssm_norm_g[l]), nbatch)
    h1t, eid, gw, cnt = _mix_route(attn, yssm, proj, x2, row(ln_in_g), row(ln_in_b), row(b_gate[l]),
                                   w_attn_out[l].astype(BF16), w_ssm_out[l].astype(BF16), w_out[l].astype(BF16),
                                   row(ln1_g[l]), row(ln1_b[l]), w_r)

    counts = cnt[0, N_EXPERT_GROUPS:N_EXPERT_GROUPS + N_EXPERTS]
    block_e, src8, dst8, n_y_tiles = _route_metadata(eid[:, :2], counts, t)
    y = _moe(block_e, src8, dst8, h1t, w_gate_e[l], w_up_e[l], w_down_e[l], n_y_tiles)
    out = _final_ln(h1t, y, gw, row(ln2_g[l]), row(ln2_b[l]))
    return out.reshape(nbatch, seq, d)
```

```python
import functools
import math

import jax
import jax.numpy as jnp
from jax import lax
from jax.experimental import pallas as pl
from jax.experimental.pallas import tpu as pltpu

F32 = jnp.float32
BF16 = jnp.bfloat16
I32 = jnp.int32
U32 = jnp.uint32

D_MODEL = 1024
SEQ = 2048
HEAD_DIM = 64
N_Q_HEADS = 16
N_KV_HEADS = 2
ATTN_BLOCK = 128
WINDOW = 128
NUM_BUCKETS = 32
MAX_DISTANCE = 128
D_INNER = 2048
N_SSM_HEADS = 32
N_SSM_GROUPS = 4
D_STATE = 128
CONV_WIDTH = 4
CHUNK = 128
N_EXPERT_GROUPS = 4
EXPERTS_PER_GROUP = 8
N_EXPERTS = 32
D_EXPERT = 512
DEEPNORM_ALPHA = 2.0 ** 0.25
LN_EPS = 1e-5

LANES = 128
SUBLANES = 8

COL_Z = 0
COL_XS = 2048
COL_B = 4096
COL_C = 4608
COL_Q = 5120
COL_GA = 6144
COL_GS = 7168
COL_KV = 8192
N_PROJ = 8704

TM_PROJ = 512
CH_PROJ = 256
CONV_CH = D_INNER + 2 * N_SSM_GROUPS * D_STATE
TM_MIX = 256
TM_FIN = 512
BM_MOE = 256
HALF_MOE = BM_MOE // 2
DMA_PRIORITIES = 2

VMEM_LIMIT = 56 * 1024 * 1024


def _layer_norm(x, g, b):
    mu = jnp.mean(x, axis=-1, keepdims=True)
    xc = x - mu
    var = jnp.mean(xc * xc, axis=-1, keepdims=True)
    return xc * lax.rsqrt(var + LN_EPS) * g + b


assert D_MODEL == SUBLANES * LANES


def _store_token_tiles(ref, val, n):
    for c in range(SUBLANES):
        ref[pl.ds(c, n, stride=SUBLANES), :] = val[:, c * LANES:(c + 1) * LANES]


def _load_token_tiles(ref, n):
    return jnp.concatenate([ref[pl.ds(c, n, stride=SUBLANES), :] for c in range(SUBLANES)], axis=1)


def _softplus(x):
    return jnp.maximum(x, 0.0) + jnp.log1p(jnp.exp(-jnp.abs(x)))


def _silu(x):
    return x * jax.nn.sigmoid(x)


def _ln_inproj_kernel(x_ref, g_ref, b_ref, w_hbm, wdt_ref, cw_ref, cb_ref, bg_ref, dtb_ref,
                      o_ref, dt_ref, w_scr, carry_scr, sem):
    i = pl.program_id(0)

    @pl.when(i == 0)
    def _():
        cp = pltpu.make_async_copy(w_hbm, w_scr, sem)
        cp.start()
        carry_scr[...] = jnp.zeros_like(carry_scr)
        cp.wait()

    hb = _layer_norm(x_ref[...], g_ref[...], b_ref[...]).astype(BF16)
    tm = hb.shape[0]

    seq_start = lax.rem(i, SEQ // TM_PROJ) == 0
    row8 = lax.broadcasted_iota(I32, (SUBLANES, CH_PROJ), 0)

    def conv_silu(r, c0):
        cc = slice(c0 - COL_XS, c0 - COL_XS + CH_PROJ)
        prev = jnp.where(seq_start, 0.0, carry_scr[:, cc])
        acc = cb_ref[:, cc] + cw_ref[CONV_WIDTH - 1:CONV_WIDTH, cc] * r
        for k in range(1, CONV_WIDTH):
            rk = pltpu.roll(r, k, axis=0)
            top = jnp.where(row8 < k, pltpu.roll(prev, k, axis=0), rk[:SUBLANES])
            rk = jnp.concatenate([top, rk[SUBLANES:]], axis=0)
            acc = acc + cw_ref[CONV_WIDTH - 1 - k:CONV_WIDTH - k, cc] * rk
        carry_scr[:, cc] = r[tm - SUBLANES:]
        return _silu(acc)

    def epilogue(r, c0):
        if c0 < COL_XS:
            return _silu(r)
        if c0 < COL_Q:
            return conv_silu(r, c0)
        if COL_GA <= c0 < COL_KV:
            return jax.nn.sigmoid(r + bg_ref[:, c0 - COL_GA:c0 - COL_GA + CH_PROJ])
        return r

    def proj(c0):
        return jnp.dot(hb, w_scr[:, c0:c0 + CH_PROJ], preferred_element_type=F32)

    starts = list(range(0, N_PROJ, CH_PROJ))
    r_next = proj(starts[0])
    for n, c0 in enumerate(starts):
        r = r_next
        if n + 1 < len(starts):
            r_next = proj(starts[n + 1])
        o_ref[:, c0:c0 + CH_PROJ] = epilogue(r, c0).astype(BF16)
    dt_ref[...] = _softplus(jnp.dot(hb, wdt_ref[...], preferred_element_type=F32) + dtb_ref[...])


def _ln_inproj(x2, g, b, w_cat, w_dt, conv_w, conv_b, b_gate, dt_bias):
    t = x2.shape[0]
    const = lambda i: (0, 0)
    return pl.pallas_call(
        _ln_inproj_kernel,
        grid=(t // TM_PROJ,),
        in_specs=[
            pl.BlockSpec((TM_PROJ, D_MODEL), lambda i: (i, 0)),
            pl.BlockSpec((1, D_MODEL), const),
            pl.BlockSpec((1, D_MODEL), const),
            pl.BlockSpec(memory_space=pl.ANY),
            pl.BlockSpec((D_MODEL, LANES), const),
            pl.BlockSpec((CONV_WIDTH, CONV_CH), const),
            pl.BlockSpec((1, CONV_CH), const),
            pl.BlockSpec((1, 2 * D_MODEL), const),
            pl.BlockSpec((1, LANES), const),
        ],
        out_specs=[
            pl.BlockSpec((TM_PROJ, N_PROJ), lambda i: (i, 0)),
            pl.BlockSpec((TM_PROJ, LANES), lambda i: (i, 0)),
        ],
        out_shape=[
            jax.ShapeDtypeStruct((t, N_PROJ), BF16),
            jax.ShapeDtypeStruct((t, LANES), F32),
        ],
        scratch_shapes=[
            pltpu.VMEM((D_MODEL, N_PROJ), BF16),
            pltpu.VMEM((SUBLANES, CONV_CH), F32),
            pltpu.SemaphoreType.DMA(()),
        ],
        compiler_params=pltpu.CompilerParams(
            dimension_semantics=("arbitrary",), vmem_limit_bytes=VMEM_LIMIT),
        name="ln_inproj",
    )(x2, g, b, w_cat, w_dt, conv_w, conv_b, b_gate, dt_bias)


def _attn_kernel(sink_ref, q_ref, kvc_ref, kvp_ref, bias_ref, o_ref):
    first = jnp.where(pl.program_id(1) == 0, 1, 0)
    lo = lax.broadcasted_iota(I32, (ATTN_BLOCK, LANES), 1) < HEAD_DIM
    for kv in range(N_KV_HEADS):
        ks = slice(kv * LANES, (kv + 1) * LANES)
        vs = slice((N_KV_HEADS + kv) * LANES, (N_KV_HEADS + kv + 1) * LANES)
        kcat = jnp.concatenate([kvp_ref[:, ks], kvc_ref[:, ks]], axis=0)
        vcat = jnp.concatenate([kvp_ref[:, vs], kvc_ref[:, vs]], axis=0)
        for pp in range(4):
            p = kv * 4 + pp
            qp = q_ref[:, p * LANES:(p + 1) * LANES].astype(F32)
            qm = jnp.concatenate([jnp.where(lo, qp, 0.0), jnp.where(lo, 0.0, qp)], axis=0).astype(BF16)
            s = lax.dot_general(qm, kcat, (((1,), (1,)), ((), ())), preferred_element_type=F32)
            outs = []
            for hh in range(2):
                h = 2 * p + hh
                sh = s[hh * ATTN_BLOCK:(hh + 1) * ATTN_BLOCK] + bias_ref[first, h]
                snk = sink_ref[h]
                m = jnp.maximum(jnp.max(sh, axis=-1, keepdims=True), snk)
                pe = jnp.exp(sh - m)
                den = jnp.sum(pe, axis=-1, keepdims=True) + jnp.exp(snk - m)
                o = jnp.dot(pe.astype(BF16), vcat, preferred_element_type=F32)
                outs.append(o / den)
            o_ref[:, p * LANES:(p + 1) * LANES] = jnp.where(lo, outs[0], outs[1]).astype(BF16)


def _swa_attn(proj, bias2, sink, nbatch):
    t = proj.shape[0]
    nb = SEQ // ATTN_BLOCK
    q_blk = COL_Q // D_MODEL
    kv_blk = COL_KV // 512
    return pl.pallas_call(
        _attn_kernel,
        grid=(nbatch, nb),
        in_specs=[
            pl.BlockSpec(memory_space=pltpu.SMEM),
            pl.BlockSpec((ATTN_BLOCK, D_MODEL), lambda b, i: (b * nb + i, q_blk)),
            pl.BlockSpec((ATTN_BLOCK, 512), lambda b, i: (b * nb + i, kv_blk)),
            pl.BlockSpec((ATTN_BLOCK, 512), lambda b, i: (b * nb + jnp.maximum(i - 1, 0), kv_blk)),
            pl.BlockSpec((2, N_Q_HEADS, ATTN_BLOCK, 2 * ATTN_BLOCK), lambda b, i: (0, 0, 0, 0)),
        ],
        out_specs=pl.BlockSpec((ATTN_BLOCK, D_MODEL), lambda b, i: (b * nb + i, 0)),
        out_shape=jax.ShapeDtypeStruct((t, D_MODEL), BF16),
        compiler_params=pltpu.CompilerParams(
            dimension_semantics=("arbitrary", "arbitrary"), vmem_limit_bytes=VMEM_LIMIT),
        name="swa_attn",
    )(sink, proj, proj, proj, bias2)


def _ssd_kernel(zs_ref, xs_ref, bc_ref, dt_ref, a_ref, dsk_ref, ng_ref, y_ref, state_scr):
    @pl.when(pl.program_id(1) == 0)
    def _():
        state_scr[...] = jnp.zeros_like(state_scr)

    bm = bc_ref[:, :N_SSM_GROUPS * D_STATE]
    cm = bc_ref[:, N_SSM_GROUPS * D_STATE:]

    dt = dt_ref[...]
    da = dt * a_ref[...]
    row = lax.broadcasted_iota(I32, (CHUNK, CHUNK), 0)
    col = lax.broadcasted_iota(I32, (CHUNK, CHUNK), 1)
    causal = row >= col
    tril = jnp.where(causal, 1.0, 0.0).astype(BF16)
    hi = da.astype(BF16)
    r1 = da - hi.astype(F32)
    mid = r1.astype(BF16)
    low = (r1 - mid.astype(F32)).astype(BF16)
    acum = (jnp.dot(tril, hi, preferred_element_type=F32) + jnp.dot(tril, mid, preferred_element_type=F32)
            + jnp.dot(tril, low, preferred_element_type=F32))
    acum_t = acum.T
    dt_t = dt.T
    dtw = dt * jnp.exp(acum[CHUNK - 1:CHUNK, :] - acum)

    def over_lanes(v, e):
        return jnp.broadcast_to(v[:, e:e + 1], (CHUNK, LANES))

    def over_rows(v, e):
        return jnp.broadcast_to(v[e:e + 1, :], (CHUNK, LANES))

    lo = lax.broadcasted_iota(I32, (CHUNK, LANES), 1) < (LANES // 2)
    gw = D_INNER // N_SSM_GROUPS
    pairs = gw // LANES
    for g in range(N_SSM_GROUPS):
        c_g = cm[:, g * D_STATE:(g + 1) * D_STATE]
        b_g = bm[:, g * D_STATE:(g + 1) * D_STATE]
        cb = lax.dot_general(c_g, b_g, (((1,), (1,)), ((), ())), preferred_element_type=F32)
        cb = jnp.where(causal, cb, 0.0)
        st_g = state_scr[:, g * gw:(g + 1) * gw]
        y_off = jnp.dot(c_g, st_g.astype(BF16), preferred_element_type=F32)
        hg, xw, ea_end = [], [], []
        for jj in range(pairs):
            j = g * pairs + jj
            cs = slice(j * LANES, (j + 1) * LANES)
            xp_bf = xs_ref[:, cs]
            xp = xp_bf.astype(F32)
            ms, ea, wl = [], [], []
            for e in (2 * j, 2 * j + 1):
                a_col = over_lanes(acum, e)
                dec = jnp.exp(jnp.minimum(a_col - over_rows(acum_t, e), 0.0))
                ms.append((cb * dec * over_rows(dt_t, e)).astype(BF16))
                ea.append(jnp.exp(a_col))
                wl.append(over_lanes(dtw, e))
            zero = jnp.zeros_like(xp_bf)
            rhs = jnp.concatenate([jnp.where(lo, xp_bf, zero), jnp.where(lo, zero, xp_bf)], axis=0)
            y_diag = jnp.dot(jnp.concatenate(ms, axis=1), rhs, preferred_element_type=F32)
            ea_p = jnp.where(lo, ea[0], ea[1])
            y_p = y_diag + y_off[:, jj * LANES:(jj + 1) * LANES] * ea_p + dsk_ref[:, cs] * xp
            hg.append(y_p * zs_ref[:, cs].astype(F32))
            xw.append((xp * jnp.where(lo, wl[0], wl[1])).astype(BF16))
            ea_end.append(ea_p[CHUNK - 1:CHUNK, :])
        b_t = b_g.astype(F32).T.astype(BF16)
        upd = jnp.dot(b_t, jnp.concatenate(xw, axis=1), preferred_element_type=F32)
        state_scr[:, g * gw:(g + 1) * gw] = st_g * jnp.concatenate(ea_end, axis=1) + upd
        h_g = jnp.concatenate(hg, axis=1)
        ms_g = jnp.mean(h_g * h_g, axis=-1, keepdims=True)
        y_ref[:, g * gw:(g + 1) * gw] = (h_g * lax.rsqrt(ms_g + LN_EPS) * ng_ref[:, g * gw:(g + 1) * gw]).astype(BF16)


def _ssd(proj, dt, a_neg, dskip_b, norm_g, nbatch):
    t = proj.shape[0]
    nc = SEQ // CHUNK
    const = lambda b, c: (0, 0)
    return pl.pallas_call(
        _ssd_kernel,
        grid=(nbatch, nc),
        in_specs=[
            pl.BlockSpec((CHUNK, D_INNER), lambda b, c: (b * nc + c, COL_Z // D_INNER)),
            pl.BlockSpec((CHUNK, D_INNER), lambda b, c: (b * nc + c, COL_XS // D_INNER)),
            pl.BlockSpec((CHUNK, 1024), lambda b, c: (b * nc + c, COL_B // 1024)),
            pl.BlockSpec((CHUNK, LANES), lambda b, c: (b * nc + c, 0)),
            pl.BlockSpec((1, LANES), const),
            pl.BlockSpec((1, D_INNER), const),
            pl.BlockSpec((1, D_INNER), const),
        ],
        out_specs=pl.BlockSpec((CHUNK, D_INNER), lambda b, c: (b * nc + c, 0)),
        out_shape=jax.ShapeDtypeStruct((t, D_INNER), BF16),
        scratch_shapes=[pltpu.VMEM((D_STATE, D_INNER), F32)],
        compiler_params=pltpu.CompilerParams(
            dimension_semantics=("arbitrary", "arbitrary"), vmem_limit_bytes=VMEM_LIMIT),
        name="ssd",
    )(proj, proj, proj, dt, a_neg, dskip_b, norm_g)


def _mix_route_kernel(attn_ref, y_ref, gate_ref, x_ref, lng_ref, lnb_ref, wao_ref, wso_ref, wo_ref,
                      l1g_ref, l1b_ref, wr_ref, h1t_ref, eid_ref, gw_ref, cnt_ref):
    ab = jnp.dot(attn_ref[...], wao_ref[...], preferred_element_type=F32)
    sb = jnp.dot(y_ref[...], wso_ref[...], preferred_element_type=F32)
    gate = gate_ref[...].astype(F32)
    mixin = (gate[:, :D_MODEL] * ab + gate[:, D_MODEL:] * sb).astype(BF16)
    mix = jnp.dot(mixin, wo_ref[...], preferred_element_type=F32)
    h = _layer_norm(x_ref[...], lng_ref[...], lnb_ref[...])
    h1 = _layer_norm(DEEPNORM_ALPHA * h + mix, l1g_ref[...], l1b_ref[...])
    tm = h1.shape[0]
    _store_token_tiles(h1t_ref, h1, tm)

    logits = jnp.dot(h1.astype(BF16), wr_ref[...], preferred_element_type=F32)
    lane = lax.broadcasted_iota(I32, (tm, LANES), 1)
    neg = -jnp.inf
    lg = jnp.where(lane < N_EXPERT_GROUPS, logits, neg)
    mg = jnp.max(lg, axis=-1, keepdims=True)
    grp = jnp.min(jnp.where(lg == mg, lane, LANES), axis=-1, keepdims=True)
    p_grp = 1.0 / jnp.sum(jnp.exp(lg - mg), axis=-1, keepdims=True)
    el = lane - N_EXPERT_GROUPS
    in_grp = (el >= 0) & (el < N_EXPERTS) & (lax.shift_right_arithmetic(el, 3) == grp)
    le = jnp.where(in_grp, logits, neg)
    m1 = jnp.max(le, axis=-1, keepdims=True)
    i1 = jnp.min(jnp.where(le == m1, lane, LANES), axis=-1, keepdims=True)
    le2 = jnp.where(lane == i1, neg, le)
    m2 = jnp.max(le2, axis=-1, keepdims=True)
    i2 = jnp.min(jnp.where(le2 == m2, lane, LANES), axis=-1, keepdims=True)
    e2 = jnp.exp(m2 - m1)
    p1 = p_grp / (1.0 + e2)
    p2 = p1 * e2
    eid_ref[...] = jnp.where(lane == 0, i1 - N_EXPERT_GROUPS, jnp.where(lane == 1, i2 - N_EXPERT_GROUPS, 0))
    gw_ref[...] = jnp.where(lane == 0, p1, jnp.where(lane == 1, p2, 0.0))

    @pl.when(pl.program_id(0) == 0)
    def _():
        cnt_ref[...] = jnp.zeros_like(cnt_ref)

    picked = jnp.where((lane == i1) | (lane == i2), 1, 0)
    cnt_ref[...] += jnp.sum(picked, axis=0, keepdims=True)


def _mix_route(attn, yssm, proj, x2, lng, lnb, w_ao, w_so, w_o, l1g, l1b, w_r):
    t = x2.shape[0]
    const = lambda i: (0, 0)
    row = lambda i: (i, 0)
    return pl.pallas_call(
        _mix_route_kernel,
        grid=(t // TM_MIX,),
        in_specs=[
            pl.BlockSpec((TM_MIX, D_MODEL), row),
            pl.BlockSpec((TM_MIX, D_INNER), row),
            pl.BlockSpec((TM_MIX, 2 * D_MODEL), lambda i: (i, COL_GA // (2 * D_MODEL))),
            pl.BlockSpec((TM_MIX, D_MODEL), row),
            pl.BlockSpec((1, D_MODEL), const),
            pl.BlockSpec((1, D_MODEL), const),
            pl.BlockSpec((D_MODEL, D_MODEL), const),
            pl.BlockSpec((D_INNER, D_MODEL), const),
            pl.BlockSpec((D_MODEL, D_MODEL), const),
            pl.BlockSpec((1, D_MODEL), const),
            pl.BlockSpec((1, D_MODEL), const),
            pl.BlockSpec((D_MODEL, LANES), const),
        ],
        out_specs=[
            pl.BlockSpec((TM_MIX * SUBLANES, LANES), row),
            pl.BlockSpec((TM_MIX, LANES), row),
            pl.BlockSpec((TM_MIX, LANES), row),
            pl.BlockSpec((1, LANES), const),
        ],
        out_shape=[
            jax.ShapeDtypeStruct((t * SUBLANES, LANES), F32),
            jax.ShapeDtypeStruct((t, LANES), I32),
            jax.ShapeDtypeStruct((t, LANES), F32),
            jax.ShapeDtypeStruct((1, LANES), I32),
        ],
        compiler_params=pltpu.CompilerParams(
            dimension_semantics=("arbitrary",), vmem_limit_bytes=VMEM_LIMIT),
        name="mix_route",
    )(attn, yssm, proj, x2, lng, lnb, w_ao, w_so, w_o, l1g, l1b, w_r)


def _moe_kernel(be_ref, cur_ref, nxt_ref, dst_ref, h_hbm, wg_ref, wu_ref, wd_ref, y_hbm,
                xbuf, ybuf, wg_bf, wu_bf, wd_bf, gsem, ssem):
    i = pl.program_id(0)
    nblk = pl.num_programs(0)

    def tile(j):
        return pl.ds(j * SUBLANES, SUBLANES)

    def issue_gather(idx_ref, base, s):
        for j in range(HALF_MOE):
            r8 = pl.multiple_of(idx_ref[0, 0, base + j], SUBLANES)
            pltpu.make_async_copy(h_hbm.at[pl.ds(r8, SUBLANES), :], xbuf.at[s, tile(j), :],
                                  gsem.at[s]).start(priority=j % DMA_PRIORITIES)

    def wait_gather(s):
        pltpu.make_async_copy(h_hbm.at[pl.ds(0, HALF_MOE * SUBLANES), :], xbuf.at[s], gsem.at[s]).wait()

    def issue_scatter(base, s):
        for j in range(HALF_MOE):
            r8 = pl.multiple_of(dst_ref[0, 0, base + j], SUBLANES)
            pltpu.make_async_copy(ybuf.at[s, tile(j), :], y_hbm.at[pl.ds(r8, SUBLANES), :],
                                  ssem.at[s]).start(priority=j % DMA_PRIORITIES)

    def wait_scatter(s):
        pltpu.make_async_copy(ybuf.at[s], y_hbm.at[pl.ds(0, HALF_MOE * SUBLANES), :], ssem.at[s]).wait()

    def expert_half(s):
        x = _load_token_tiles(xbuf.at[s], HALF_MOE).astype(BF16)
        gt = jnp.dot(x, wg_bf[...], preferred_element_type=F32)
        up = jnp.dot(x, wu_bf[...], preferred_element_type=F32)
        hid = (gt * jax.nn.sigmoid(gt) * up).astype(BF16)
        y = jnp.dot(hid, wd_bf[...], preferred_element_type=F32)
        _store_token_tiles(ybuf.at[s], y, HALF_MOE)

    @pl.when(i == 0)
    def _():
        issue_gather(cur_ref, 0, 0)

    @pl.when((i == 0) | (be_ref[i] != be_ref[jnp.maximum(i - 1, 0)]))
    def _():
        wg_bf[...] = wg_ref[0].astype(BF16)
        wu_bf[...] = wu_ref[0].astype(BF16)
        wd_bf[...] = wd_ref[0].astype(BF16)

    @pl.when(i > 0)
    def _():
        wait_scatter(0)

    wait_gather(0)
    issue_gather(cur_ref, HALF_MOE, 1)
    expert_half(0)
    issue_scatter(0, 0)

    @pl.when(i > 0)
    def _():
        wait_scatter(1)

    wait_gather(1)
    issue_gather(nxt_ref, 0, 0)
    expert_half(1)
    issue_scatter(HALF_MOE, 1)

    @pl.when(i == nblk - 1)
    def _():
        wait_gather(0)
        wait_scatter(0)
        wait_scatter(1)


def _moe(block_e, src8, dst8, h1t, w_gate, w_up, w_down, n_y_tiles):
    nblk = src8.shape[0] // BM_MOE
    src3 = src8.reshape(nblk, 1, BM_MOE)
    nxt3 = jnp.concatenate([src3[1:], src3[-1:]], axis=0)
    dst3 = dst8.reshape(nblk, 1, BM_MOE)
    idx_spec = pl.BlockSpec((1, 1, BM_MOE), lambda i, be: (i, 0, 0), memory_space=pltpu.SMEM)
    grid_spec = pltpu.PrefetchScalarGridSpec(
        num_scalar_prefetch=1,
        grid=(nblk,),
        in_specs=[
            idx_spec,
            idx_spec,
            idx_spec,
            pl.BlockSpec(memory_space=pl.ANY),
            pl.BlockSpec((1, D_MODEL, D_EXPERT), lambda i, be: (be[i], 0, 0)),
            pl.BlockSpec((1, D_MODEL, D_EXPERT), lambda i, be: (be[i], 0, 0)),
            pl.BlockSpec((1, D_EXPERT, D_MODEL), lambda i, be: (be[i], 0, 0)),
        ],
        out_specs=pl.BlockSpec(memory_space=pl.ANY),
        scratch_shapes=[
            pltpu.VMEM((2, HALF_MOE * SUBLANES, LANES), F32),
            pltpu.VMEM((2, HALF_MOE * SUBLANES, LANES), F32),
            pltpu.VMEM((D_MODEL, D_EXPERT), BF16),
            pltpu.VMEM((D_MODEL, D_EXPERT), BF16),
            pltpu.VMEM((D_EXPERT, D_MODEL), BF16),
            pltpu.SemaphoreType.DMA((2,)),
            pltpu.SemaphoreType.DMA((2,)),
        ],
    )
    return pl.pallas_call(
        _moe_kernel,
        grid_spec=grid_spec,
        out_shape=jax.ShapeDtypeStruct((n_y_tiles * SUBLANES, LANES), F32),
        compiler_params=pltpu.CompilerParams(
            dimension_semantics=("arbitrary",), vmem_limit_bytes=VMEM_LIMIT),
        name="moe",
    )(block_e, src3, nxt3, dst3, h1t, w_gate, w_up, w_down)


def _final_kernel(h1t_ref, y0_ref, y1_ref, gw_ref, g_ref, b_ref, o_ref):
    n = o_ref.shape[0]
    gw = gw_ref[...]
    ffn = gw[:, 0:1] * _load_token_tiles(y0_ref, n) + gw[:, 1:2] * _load_token_tiles(y1_ref, n)
    o_ref[...] = _layer_norm(DEEPNORM_ALPHA * _load_token_tiles(h1t_ref, n) + ffn, g_ref[...], b_ref[...])


def _final_ln(h1t, y, gw, g, b):
    t = gw.shape[0]
    nt = t // TM_FIN
    const = lambda i: (0, 0)
    tiles = (TM_FIN * SUBLANES, LANES)
    return pl.pallas_call(
        _final_kernel,
        grid=(nt,),
        in_specs=[
            pl.BlockSpec(tiles, lambda i: (i, 0)),
            pl.BlockSpec(tiles, lambda i: (i, 0)),
            pl.BlockSpec(tiles, lambda i: (i + nt, 0)),
            pl.BlockSpec((TM_FIN, LANES), lambda i: (i, 0)),
            pl.BlockSpec((1, D_MODEL), const),
            pl.BlockSpec((1, D_MODEL), const),
        ],
        out_specs=pl.BlockSpec((TM_FIN, D_MODEL), lambda i: (i, 0)),
        out_shape=jax.ShapeDtypeStruct((t, D_MODEL), F32),
        compiler_params=pltpu.CompilerParams(
            dimension_semantics=("arbitrary",), vmem_limit_bytes=VMEM_LIMIT),
        name="final_ln",
    )(h1t, y, y, gw, g, b)


def _t5_causal_bucket(dist):
    max_exact = NUM_BUCKETS // 2
    d_f = jnp.maximum(dist, 1).astype(F32)
    large = max_exact + (jnp.log(d_f / max_exact) / math.log(MAX_DISTANCE / max_exact)
                         * (NUM_BUCKETS - max_exact)).astype(I32)
    large = jnp.minimum(large, NUM_BUCKETS - 1)
    return jnp.where(dist < max_exact, dist, large)


def _attn_bias_tables(rel_bias):
    qi = jnp.arange(ATTN_BLOCK)[:, None]
    kj = jnp.arange(2 * ATTN_BLOCK)[None, :]
    dist = qi + ATTN_BLOCK - kj
    bucket = _t5_causal_bucket(jnp.clip(dist, 0, None))
    hit = bucket[None, None] == jnp.arange(NUM_BUCKETS, dtype=I32)[None, :, None, None]
    bias = jnp.sum(jnp.where(hit, rel_bias.astype(F32).T[:, :, None, None], 0.0), axis=1)
    in_window = (dist >= 0) & (dist < WINDOW)
    normal = jnp.where(in_window[None], bias, -jnp.inf)
    first = jnp.where((in_window & (kj >= ATTN_BLOCK))[None], bias, -jnp.inf)
    return jnp.stack([normal, first], axis=0)


ASSIGN_BITS = 17


def _route_metadata(eid, counts, t):
    n_assign = 2 * t
    assert n_assign <= 1 << ASSIGN_BITS
    n_blk = n_assign // BM_MOE + N_EXPERTS
    padded = (counts + BM_MOE - 1) // BM_MOE * BM_MOE
    pends = jnp.cumsum(padded)
    n_pad = padded - counts
    key = (eid.reshape(-1) << (ASSIGN_BITS + 1)) | jnp.arange(n_assign, dtype=I32)
    j = jnp.arange(BM_MOE, dtype=I32)[None, :]
    ee = jnp.arange(N_EXPERTS, dtype=I32)[:, None]
    pad_key = jnp.where(j < n_pad[:, None], (ee << (ASSIGN_BITS + 1)) | (1 << ASSIGN_BITS) | j,
                        jnp.iinfo(jnp.int32).max)
    keys = jnp.sort(jnp.concatenate([key, pad_key.reshape(-1)]))
    real = (keys & (1 << ASSIGN_BITS)) == 0
    a = keys & ((1 << ASSIGN_BITS) - 1)
    tok = a >> 1
    r = jnp.arange(n_blk * BM_MOE, dtype=I32)
    src8 = jnp.where(real, tok, 0) * SUBLANES
    dst8 = jnp.where(real, (a & 1) * t + tok, n_assign + (r % BM_MOE)) * SUBLANES
    blk_start = jnp.arange(n_blk, dtype=I32)[:, None] * BM_MOE
    block_e = jnp.minimum(jnp.sum((pends[None, :] <= blk_start).astype(I32), axis=1), N_EXPERTS - 1)
    return block_e.astype(I32), src8.astype(I32), dst8.astype(I32), n_assign + BM_MOE


def kernel(x, ln_in_g, ln_in_b, rel_bias, w_in, b_gate, attn_sink, conv_w, conv_b, dt_bias, a_log, d_skip,
           ssm_norm_g, w_attn_out, w_ssm_out, w_out, ln1_g, ln1_b, w_group_router, w_expert_router,
           w_gate_e, w_up_e, w_down_e, ln2_g, ln2_b):
    nbatch, seq, d = x.shape
    assert seq == SEQ and d == D_MODEL
    t = nbatch * seq
    x2 = x.reshape(t, d)
    l = 0

    w = w_in[l]
    q_w, k_w, v_w, z_w, xs_w, b_w, c_w, dt_w, ga_w, gs_w = jnp.split(
        w, [1024, 1152, 1280, 3328, 5376, 5888, 6400, 6432, 7456], axis=1)
    k0, k1 = k_w[:, :HEAD_DIM], k_w[:, HEAD_DIM:]
    v0, v1 = v_w[:, :HEAD_DIM], v_w[:, HEAD_DIM:]
    w_cat = jnp.concatenate([z_w, xs_w, b_w, c_w, q_w * (HEAD_DIM ** -0.5), ga_w, gs_w,
                             k0, k0, k1, k1, v0, v0, v1, v1], axis=1).astype(BF16)
    w_dt = jnp.pad(dt_w, ((0, 0), (0, LANES - N_SSM_HEADS))).astype(BF16)
    row = lambda v: v.reshape(1, -1).astype(F32)
    pad_heads = lambda v: jnp.pad(v.astype(F32), (0, LANES - N_SSM_HEADS)).reshape(1, LANES)
    a_neg = pad_heads(-jnp.exp(a_log[l].astype(F32)))
    dskip_b = jnp.repeat(d_skip[l].astype(F32), D_INNER // N_SSM_HEADS).reshape(1, D_INNER)
    w_r = jnp.pad(jnp.concatenate([w_group_router[l], w_expert_router[l]], axis=1),
                  ((0, 0), (0, LANES - N_EXPERT_GROUPS - N_EXPERTS))).astype(BF16)
    bias2 = _attn_bias_tables(rel_bias)

    proj, dt = _ln_inproj(x2, row(ln_in_g), row(ln_in_b), w_cat, w_dt, conv_w[l].astype(F32), row(conv_b[l]),
                          row(b_gate[l]), pad_heads(dt_bias[l]))
    attn = _swa_attn(proj, bias2, attn_sink[l].astype(F32), nbatch)
    yssm = _ssd(proj, dt, a_neg, dskip_b, row(ssm_norm_g[l]), nbatch)
    h1t, eid, gw, cnt = _mix_route(attn, yssm, proj, x2, row(ln_in_g), row(ln_in_b),
                                   w_attn_out[l].astype(BF16), w_ssm_out[l].astype(BF16), w_out[l].astype(BF16),
                                   row(ln1_g[l]), row(ln1_b[l]), w_r)

    counts = cnt[0, N_EXPERT_GROUPS:N_EXPERT_GROUPS + N_EXPERTS]
    block_e, src8, dst8, n_y_tiles = _route_metadata(eid[:, :2], counts, t)
    y = _moe(block_e, src8, dst8, h1t, w_gate_e[l], w_up_e[l], w_down_e[l], n_y_tiles)
    out = _final_ln(h1t, y, gw, row(ln2_g[l]), row(ln2_b[l]))
    return out.reshape(nbatch, seq, d)
```

```python
import functools
import math

import jax
import jax.numpy as jnp
from jax import lax
from jax.experimental import pallas as pl
from jax.experimental.pallas import tpu as pltpu

F32 = jnp.float32
BF16 = jnp.bfloat16
I32 = jnp.int32
U32 = jnp.uint32

D_MODEL = 1024
SEQ = 2048
HEAD_DIM = 64
N_Q_HEADS = 16
N_KV_HEADS = 2
ATTN_BLOCK = 128
WINDOW = 128
NUM_BUCKETS = 32
MAX_DISTANCE = 128
D_INNER = 2048
N_SSM_HEADS = 32
N_SSM_GROUPS = 4
D_STATE = 128
CONV_WIDTH = 4
CHUNK = 128
N_EXPERT_GROUPS = 4
EXPERTS_PER_GROUP = 8
N_EXPERTS = 32
D_EXPERT = 512
DEEPNORM_ALPHA = 2.0 ** 0.25
LN_EPS = 1e-5

LANES = 128
SUBLANES = 8

COL_Z = 0
COL_XS = 2048
COL_B = 4096
COL_C = 4608
COL_Q = 5120
COL_GA = 6144
COL_GS = 7168
COL_KV = 8192
N_PROJ = 8704

TM_PROJ = 512
CH_PROJ = 256
CONV_CH = D_INNER + 2 * N_SSM_GROUPS * D_STATE
RAW_SLOTS = 3
TM_MIX = 256
TM_FIN = 512
BM_MOE = 512
HALF_MOE = 128
PARTS_MOE = BM_MOE // HALF_MOE
GATHER_LEAD = 2
DMA_PRIORITIES = 2

VMEM_LIMIT = 56 * 1024 * 1024


def _layer_norm(x, g, b):
    mu = jnp.mean(x, axis=-1, keepdims=True)
    xc = x - mu
    var = jnp.mean(xc * xc, axis=-1, keepdims=True)
    return xc * lax.rsqrt(var + LN_EPS) * g + b


assert D_MODEL == SUBLANES * LANES


def _store_token_tiles(ref, val, n):
    for c in range(SUBLANES):
        ref[pl.ds(c, n, stride=SUBLANES), :] = val[:, c * LANES:(c + 1) * LANES]


def _load_token_tiles(ref, n):
    return jnp.concatenate([ref[pl.ds(c, n, stride=SUBLANES), :] for c in range(SUBLANES)], axis=1)


def _softplus(x):
    return jnp.maximum(x, 0.0) + jnp.log1p(jnp.exp(-jnp.abs(x)))


def _silu(x):
    return x * jax.nn.sigmoid(x)


def _ln_inproj_kernel(x_ref, g_ref, b_ref, w_hbm, wdt_ref, cw_ref, cb_ref, bg_ref, dtb_ref,
                      o_ref, dt_ref, w_scr, carry_scr, raw_scr, sem):
    i = pl.program_id(0)

    @pl.when(i == 0)
    def _():
        cp = pltpu.make_async_copy(w_hbm, w_scr, sem)
        cp.start()
        carry_scr[...] = jnp.zeros_like(carry_scr)
        cp.wait()

    hb = _layer_norm(x_ref[...], g_ref[...], b_ref[...]).astype(BF16)
    tm = hb.shape[0]

    seq_start = lax.rem(i, SEQ // TM_PROJ) == 0
    sub = lax.broadcasted_iota(I32, (tm // SUBLANES, SUBLANES, CH_PROJ), 1)

    def conv_silu(r, c0):
        cc = slice(c0 - COL_XS, c0 - COL_XS + CH_PROJ)
        prev = jnp.where(seq_start, 0.0, carry_scr[:, cc])
        carry_scr[:, cc] = r[tm - SUBLANES:]
        acc = cb_ref[:, cc] + cw_ref[CONV_WIDTH - 1:CONV_WIDTH, cc] * r
        r3 = jnp.concatenate([prev, r], axis=0).reshape(tm // SUBLANES + 1, SUBLANES, CH_PROJ)
        for k in range(1, CONV_WIDTH):
            rot = pltpu.roll(r3, k, axis=1)
            rk = jnp.where(sub < k, rot[:-1], rot[1:]).reshape(tm, CH_PROJ)
            acc = acc + cw_ref[CONV_WIDTH - 1 - k:CONV_WIDTH - k, cc] * rk
        return _silu(acc)

    def epilogue(r, c0):
        if c0 < COL_XS:
            return _silu(r)
        if c0 < COL_Q:
            return conv_silu(r, c0)
        if COL_GA <= c0 < COL_KV:
            return jax.nn.sigmoid(r + bg_ref[:, c0 - COL_GA:c0 - COL_GA + CH_PROJ])
        return r

    def proj(c0):
        return jnp.dot(hb, w_scr[:, c0:c0 + CH_PROJ], preferred_element_type=F32)

    heavy = list(range(COL_XS, COL_Q, CH_PROJ))
    light = [c for c in range(0, N_PROJ, CH_PROJ) if c not in heavy]
    starts = []
    while heavy or light:
        starts += heavy[:1] + light[:2]
        heavy, light = heavy[1:], light[2:]
    zero = jnp.minimum(i, 0)
    ahead = RAW_SLOTS - 1
    for n in range(ahead):
        raw_scr[zero + n] = proj(starts[n])
    for n, c0 in enumerate(starts):
        if n + ahead < len(starts):
            raw_scr[zero + (n + ahead) % RAW_SLOTS] = proj(starts[n + ahead])
        o_ref[:, c0:c0 + CH_PROJ] = epilogue(raw_scr[zero + n % RAW_SLOTS], c0).astype(BF16)
    dt_ref[...] = _softplus(jnp.dot(hb, wdt_ref[...], preferred_element_type=F32) + dtb_ref[...])


def _ln_inproj(x2, g, b, w_cat, w_dt, conv_w, conv_b, b_gate, dt_bias):
    t = x2.shape[0]
    const = lambda i: (0, 0)
    return pl.pallas_call(
        _ln_inproj_kernel,
        grid=(t // TM_PROJ,),
        in_specs=[
            pl.BlockSpec((TM_PROJ, D_MODEL), lambda i: (i, 0)),
            pl.BlockSpec((1, D_MODEL), const),
            pl.BlockSpec((1, D_MODEL), const),
            pl.BlockSpec(memory_space=pl.ANY),
            pl.BlockSpec((D_MODEL, LANES), const),
            pl.BlockSpec((CONV_WIDTH, CONV_CH), const),
            pl.BlockSpec((1, CONV_CH), const),
            pl.BlockSpec((1, 2 * D_MODEL), const),
            pl.BlockSpec((1, LANES), const),
        ],
        out_specs=[
            pl.BlockSpec((TM_PROJ, N_PROJ), lambda i: (i, 0)),
            pl.BlockSpec((TM_PROJ, LANES), lambda i: (i, 0)),
        ],
        out_shape=[
            jax.ShapeDtypeStruct((t, N_PROJ), BF16),
            jax.ShapeDtypeStruct((t, LANES), F32),
        ],
        scratch_shapes=[
            pltpu.VMEM((D_MODEL, N_PROJ), BF16),
            pltpu.VMEM((SUBLANES, CONV_CH), F32),
            pltpu.VMEM((RAW_SLOTS, TM_PROJ, CH_PROJ), F32),
            pltpu.SemaphoreType.DMA(()),
        ],
        compiler_params=pltpu.CompilerParams(
            dimension_semantics=("arbitrary",), vmem_limit_bytes=VMEM_LIMIT),
        name="ln_inproj",
    )(x2, g, b, w_cat, w_dt, conv_w, conv_b, b_gate, dt_bias)


def _attn_kernel(sink_ref, q_ref, kvc_ref, kvp_ref, bias_ref, o_ref):
    first = jnp.where(pl.program_id(1) == 0, 1, 0)
    lo = lax.broadcasted_iota(I32, (ATTN_BLOCK, LANES), 1) < HEAD_DIM
    for kv in range(N_KV_HEADS):
        ks = slice(kv * LANES, (kv + 1) * LANES)
        vs = slice((N_KV_HEADS + kv) * LANES, (N_KV_HEADS + kv + 1) * LANES)
        kcat = jnp.concatenate([kvp_ref[:, ks], kvc_ref[:, ks]], axis=0)
        vcat = jnp.concatenate([kvp_ref[:, vs], kvc_ref[:, vs]], axis=0)
        for pp in range(4):
            p = kv * 4 + pp
            qp = q_ref[:, p * LANES:(p + 1) * LANES].astype(F32)
            qm = jnp.concatenate([jnp.where(lo, qp, 0.0), jnp.where(lo, 0.0, qp)], axis=0).astype(BF16)
            s = lax.dot_general(qm, kcat, (((1,), (1,)), ((), ())), preferred_element_type=F32)
            outs = []
            for hh in range(2):
                h = 2 * p + hh
                sh = s[hh * ATTN_BLOCK:(hh + 1) * ATTN_BLOCK] + bias_ref[first, h]
                snk = sink_ref[h]
                m = jnp.maximum(jnp.max(sh, axis=-1, keepdims=True), snk)
                pe = jnp.exp(sh - m)
                den = jnp.sum(pe, axis=-1, keepdims=True) + jnp.exp(snk - m)
                o = jnp.dot(pe.astype(BF16), vcat, preferred_element_type=F32)
                outs.append(o / den)
            o_ref[:, p * LANES:(p + 1) * LANES] = jnp.where(lo, outs[0], outs[1]).astype(BF16)


def _swa_attn(proj, bias2, sink, nbatch):
    t = proj.shape[0]
    nb = SEQ // ATTN_BLOCK
    q_blk = COL_Q // D_MODEL
    kv_blk = COL_KV // 512
    return pl.pallas_call(
        _attn_kernel,
        grid=(nbatch, nb),
        in_specs=[
            pl.BlockSpec(memory_space=pltpu.SMEM),
            pl.BlockSpec((ATTN_BLOCK, D_MODEL), lambda b, i: (b * nb + i, q_blk)),
            pl.BlockSpec((ATTN_BLOCK, 512), lambda b, i: (b * nb + i, kv_blk)),
            pl.BlockSpec((ATTN_BLOCK, 512), lambda b, i: (b * nb + jnp.maximum(i - 1, 0), kv_blk)),
            pl.BlockSpec((2, N_Q_HEADS, ATTN_BLOCK, 2 * ATTN_BLOCK), lambda b, i: (0, 0, 0, 0)),
        ],
        out_specs=pl.BlockSpec((ATTN_BLOCK, D_MODEL), lambda b, i: (b * nb + i, 0)),
        out_shape=jax.ShapeDtypeStruct((t, D_MODEL), BF16),
        compiler_params=pltpu.CompilerParams(
            dimension_semantics=("arbitrary", "arbitrary"), vmem_limit_bytes=VMEM_LIMIT),
        name="swa_attn",
    )(sink, proj, proj, proj, bias2)


def _ssd_kernel(zs_ref, xs_ref, bc_ref, dt_ref, a_ref, dsk_ref, ng_ref, y_ref, state_scr):
    @pl.when(pl.program_id(1) == 0)
    def _():
        state_scr[...] = jnp.zeros_like(state_scr)

    bm = bc_ref[:, :N_SSM_GROUPS * D_STATE]
    cm = bc_ref[:, N_SSM_GROUPS * D_STATE:]

    dt = dt_ref[...]
    da = dt * a_ref[...]
    row = lax.broadcasted_iota(I32, (CHUNK, CHUNK), 0)
    col = lax.broadcasted_iota(I32, (CHUNK, CHUNK), 1)
    causal = row >= col
    tril = jnp.where(causal, 1.0, 0.0).astype(BF16)
    hi = da.astype(BF16)
    r1 = da - hi.astype(F32)
    mid = r1.astype(BF16)
    low = (r1 - mid.astype(F32)).astype(BF16)
    acum = (jnp.dot(tril, hi, preferred_element_type=F32) + jnp.dot(tril, mid, preferred_element_type=F32)
            + jnp.dot(tril, low, preferred_element_type=F32))
    acum_t = acum.T
    dt_t = dt.T
    dtw = dt * jnp.exp(acum[CHUNK - 1:CHUNK, :] - acum)

    def over_lanes(v, e):
        return jnp.broadcast_to(v[:, e:e + 1], (CHUNK, LANES))

    def over_rows(v, e):
        return jnp.broadcast_to(v[e:e + 1, :], (CHUNK, LANES))

    lo = lax.broadcasted_iota(I32, (CHUNK, LANES), 1) < (LANES // 2)
    gw = D_INNER // N_SSM_GROUPS
    pairs = gw // LANES
    for g in range(N_SSM_GROUPS):
        c_g = cm[:, g * D_STATE:(g + 1) * D_STATE]
        b_g = bm[:, g * D_STATE:(g + 1) * D_STATE]
        cb = lax.dot_general(c_g, b_g, (((1,), (1,)), ((), ())), preferred_element_type=F32)
        cb = jnp.where(causal, cb, 0.0)
        st_g = state_scr[:, g * gw:(g + 1) * gw]
        y_off = jnp.dot(c_g, st_g.astype(BF16), preferred_element_type=F32)
        hg, xw, ea_end = [], [], []
        for jj in range(pairs):
            j = g * pairs + jj
            cs = slice(j * LANES, (j + 1) * LANES)
            xp_bf = xs_ref[:, cs]
            xp = xp_bf.astype(F32)
            ms, ea, wl = [], [], []
            for e in (2 * j, 2 * j + 1):
                a_col = over_lanes(acum, e)
                dec = jnp.exp(jnp.minimum(a_col - over_rows(acum_t, e), 0.0))
                ms.append((cb * dec * over_rows(dt_t, e)).astype(BF16))
                ea.append(jnp.exp(a_col))
                wl.append(over_lanes(dtw, e))
            zero = jnp.zeros_like(xp_bf)
            rhs = jnp.concatenate([jnp.where(lo, xp_bf, zero), jnp.where(lo, zero, xp_bf)], axis=0)
            y_diag = jnp.dot(jnp.concatenate(ms, axis=1), rhs, preferred_element_type=F32)
            ea_p = jnp.where(lo, ea[0], ea[1])
            y_p = y_diag + y_off[:, jj * LANES:(jj + 1) * LANES] * ea_p + dsk_ref[:, cs] * xp
            hg.append(y_p * zs_ref[:, cs].astype(F32))
            xw.append((xp * jnp.where(lo, wl[0], wl[1])).astype(BF16))
            ea_end.append(ea_p[CHUNK - 1:CHUNK, :])
        b_t = b_g.astype(F32).T.astype(BF16)
        upd = jnp.dot(b_t, jnp.concatenate(xw, axis=1), preferred_element_type=F32)
        state_scr[:, g * gw:(g + 1) * gw] = st_g * jnp.concatenate(ea_end, axis=1) + upd
        h_g = jnp.concatenate(hg, axis=1)
        ms_g = jnp.mean(h_g * h_g, axis=-1, keepdims=True)
        y_ref[:, g * gw:(g + 1) * gw] = (h_g * lax.rsqrt(ms_g + LN_EPS) * ng_ref[:, g * gw:(g + 1) * gw]).astype(BF16)


def _ssd(proj, dt, a_neg, dskip_b, norm_g, nbatch):
    t = proj.shape[0]
    nc = SEQ // CHUNK
    const = lambda b, c: (0, 0)
    return pl.pallas_call(
        _ssd_kernel,
        grid=(nbatch, nc),
        in_specs=[
            pl.BlockSpec((CHUNK, D_INNER), lambda b, c: (b * nc + c, COL_Z // D_INNER)),
            pl.BlockSpec((CHUNK, D_INNER), lambda b, c: (b * nc + c, COL_XS // D_INNER)),
            pl.BlockSpec((CHUNK, 1024), lambda b, c: (b * nc + c, COL_B // 1024)),
            pl.BlockSpec((CHUNK, LANES), lambda b, c: (b * nc + c, 0)),
            pl.BlockSpec((1, LANES), const),
            pl.BlockSpec((1, D_INNER), const),
            pl.BlockSpec((1, D_INNER), const),
        ],
        out_specs=pl.BlockSpec((CHUNK, D_INNER), lambda b, c: (b * nc + c, 0)),
        out_shape=jax.ShapeDtypeStruct((t, D_INNER), BF16),
        scratch_shapes=[pltpu.VMEM((D_STATE, D_INNER), F32)],
        compiler_params=pltpu.CompilerParams(
            dimension_semantics=("arbitrary", "arbitrary"), vmem_limit_bytes=VMEM_LIMIT),
        name="ssd",
    )(proj, proj, proj, dt, a_neg, dskip_b, norm_g)


def _mix_route_kernel(attn_ref, y_ref, gate_ref, x_ref, lng_ref, lnb_ref, wao_ref, wso_ref, wo_ref,
                      l1g_ref, l1b_ref, wr_ref, h1t_ref, eid_ref, gw_ref, cnt_ref, logit_scr):
    i = pl.program_id(0)

    @pl.when(i == 0)
    def _():
        cnt_ref[...] = jnp.zeros_like(cnt_ref)
        logit_scr[...] = jnp.zeros_like(logit_scr)

    logits = logit_scr[...]
    lane = lax.broadcasted_iota(I32, (TM_MIX, LANES), 1)
    neg = -jnp.inf
    lg = jnp.where(lane < N_EXPERT_GROUPS, logits, neg)
    mg = jnp.max(lg, axis=-1, keepdims=True)
    grp = jnp.min(jnp.where(lg == mg, lane, LANES), axis=-1, keepdims=True)
    p_grp = 1.0 / jnp.sum(jnp.exp(lg - mg), axis=-1, keepdims=True)
    el = lane - N_EXPERT_GROUPS
    in_grp = (el >= 0) & (el < N_EXPERTS) & (lax.shift_right_arithmetic(el, 3) == grp)
    le = jnp.where(in_grp, logits, neg)
    m1 = jnp.max(le, axis=-1, keepdims=True)
    i1 = jnp.min(jnp.where(le == m1, lane, LANES), axis=-1, keepdims=True)
    le2 = jnp.where(lane == i1, neg, le)
    m2 = jnp.max(le2, axis=-1, keepdims=True)
    i2 = jnp.min(jnp.where(le2 == m2, lane, LANES), axis=-1, keepdims=True)
    e2 = jnp.exp(m2 - m1)
    p1 = p_grp / (1.0 + e2)
    p2 = p1 * e2
    eid_ref[...] = jnp.where(lane == 0, i1 - N_EXPERT_GROUPS, jnp.where(lane == 1, i2 - N_EXPERT_GROUPS, 0))
    gw_ref[...] = jnp.where(lane == 0, p1, jnp.where(lane == 1, p2, 0.0))
    picked = jnp.where(((lane == i1) | (lane == i2)) & (i > 0), 1, 0)
    cnt_ref[...] += jnp.sum(picked, axis=0, keepdims=True)

    ab = jnp.dot(attn_ref[...], wao_ref[...], preferred_element_type=F32)
    sb = jnp.dot(y_ref[...], wso_ref[...], preferred_element_type=F32)
    gate = gate_ref[...].astype(F32)
    mixin = (gate[:, :D_MODEL] * ab + gate[:, D_MODEL:] * sb).astype(BF16)
    mix = jnp.dot(mixin, wo_ref[...], preferred_element_type=F32)
    h = _layer_norm(x_ref[...], lng_ref[...], lnb_ref[...])
    h1 = _layer_norm(DEEPNORM_ALPHA * h + mix, l1g_ref[...], l1b_ref[...])
    _store_token_tiles(h1t_ref, h1, TM_MIX)
    logit_scr[...] = jnp.dot(h1.astype(BF16), wr_ref[...], preferred_element_type=F32)


def _mix_route(attn, yssm, proj, x2, lng, lnb, w_ao, w_so, w_o, l1g, l1b, w_r):
    t = x2.shape[0]
    nt = t // TM_MIX
    const = lambda i: (0, 0)
    row = lambda i: (jnp.minimum(i, nt - 1), 0)
    prev = lambda i: (jnp.maximum(i - 1, 0), 0)
    return pl.pallas_call(
        _mix_route_kernel,
        grid=(nt + 1,),
        in_specs=[
            pl.BlockSpec((TM_MIX, D_MODEL), row),
            pl.BlockSpec((TM_MIX, D_INNER), row),
            pl.BlockSpec((TM_MIX, 2 * D_MODEL), lambda i: (jnp.minimum(i, nt - 1), COL_GA // (2 * D_MODEL))),
            pl.BlockSpec((TM_MIX, D_MODEL), row),
            pl.BlockSpec((1, D_MODEL), const),
            pl.BlockSpec((1, D_MODEL), const),
            pl.BlockSpec((D_MODEL, D_MODEL), const),
            pl.BlockSpec((D_INNER, D_MODEL), const),
            pl.BlockSpec((D_MODEL, D_MODEL), const),
            pl.BlockSpec((1, D_MODEL), const),
            pl.BlockSpec((1, D_MODEL), const),
            pl.BlockSpec((D_MODEL, LANES), const),
        ],
        out_specs=[
            pl.BlockSpec((TM_MIX * SUBLANES, LANES), row),
            pl.BlockSpec((TM_MIX, LANES), prev),
            pl.BlockSpec((TM_MIX, LANES), prev),
            pl.BlockSpec((1, LANES), const),
        ],
        out_shape=[
            jax.ShapeDtypeStruct((t * SUBLANES, LANES), F32),
            jax.ShapeDtypeStruct((t, LANES), I32),
            jax.ShapeDtypeStruct((t, LANES), F32),
            jax.ShapeDtypeStruct((1, LANES), I32),
        ],
        scratch_shapes=[pltpu.VMEM((TM_MIX, LANES), F32)],
        compiler_params=pltpu.CompilerParams(
            dimension_semantics=("arbitrary",), vmem_limit_bytes=VMEM_LIMIT),
        name="mix_route",
    )(attn, yssm, proj, x2, lng, lnb, w_ao, w_so, w_o, l1g, l1b, w_r)


def _moe_kernel(be_ref, cur_ref, nxt_ref, dst_ref, h_hbm, wg_ref, wu_ref, wd_ref, y_hbm,
                xbuf, ybuf, wg_bf, wu_bf, wd_bf, gsem, ssem):
    i = pl.program_id(0)
    nblk = pl.num_programs(0)

    def tile(j):
        return pl.ds(j * SUBLANES, SUBLANES)

    def issue_gather(idx_ref, base, s):
        for j in range(HALF_MOE):
            r8 = pl.multiple_of(idx_ref[0, 0, base + j], SUBLANES)
            pltpu.make_async_copy(h_hbm.at[pl.ds(r8, SUBLANES), :], xbuf.at[s, tile(j), :],
                                  gsem.at[s]).start(priority=j % DMA_PRIORITIES)

    def wait_gather(s):
        pltpu.make_async_copy(h_hbm.at[pl.ds(0, HALF_MOE * SUBLANES), :], xbuf.at[s], gsem.at[s]).wait()

    def issue_scatter(base, s):
        for j in range(HALF_MOE):
            r8 = pl.multiple_of(dst_ref[0, 0, base + j], SUBLANES)
            pltpu.make_async_copy(ybuf.at[s, tile(j), :], y_hbm.at[pl.ds(r8, SUBLANES), :],
                                  ssem.at[s]).start(priority=j % DMA_PRIORITIES)

    def wait_scatter(s):
        pltpu.make_async_copy(ybuf.at[s], y_hbm.at[pl.ds(0, HALF_MOE * SUBLANES), :], ssem.at[s]).wait()

    def expert_half(s):
        x = _load_token_tiles(xbuf.at[s], HALF_MOE).astype(BF16)
        gt = jnp.dot(x, wg_bf[...], preferred_element_type=F32)
        up = jnp.dot(x, wu_bf[...], preferred_element_type=F32)
        hid = (gt * jax.nn.sigmoid(gt) * up).astype(BF16)
        y = jnp.dot(hid, wd_bf[...], preferred_element_type=F32)
        _store_token_tiles(ybuf.at[s], y, HALF_MOE)

    @pl.when(i == 0)
    def _():
        for s in range(GATHER_LEAD):
            issue_gather(cur_ref, s * HALF_MOE, s)

    @pl.when((i == 0) | (be_ref[i] != be_ref[jnp.maximum(i - 1, 0)]))
    def _():
        wg_bf[...] = wg_ref[0].astype(BF16)
        wu_bf[...] = wu_ref[0].astype(BF16)
        wd_bf[...] = wd_ref[0].astype(BF16)

    for s in range(PARTS_MOE):
        @pl.when(i > 0)
        def _():
            wait_scatter(s)

        wait_gather(s)
        ahead = s + GATHER_LEAD
        if ahead < PARTS_MOE:
            issue_gather(cur_ref, ahead * HALF_MOE, ahead)
        else:
            issue_gather(nxt_ref, (ahead - PARTS_MOE) * HALF_MOE, ahead - PARTS_MOE)
        expert_half(s)
        issue_scatter(s * HALF_MOE, s)

    @pl.when(i == nblk - 1)
    def _():
        for s in range(GATHER_LEAD):
            wait_gather(s)
        for s in range(PARTS_MOE):
            wait_scatter(s)


def _moe(block_e, src8, dst8, h1t, w_gate, w_up, w_down, n_y_tiles):
    nblk = src8.shape[0] // BM_MOE
    src3 = src8.reshape(nblk, 1, BM_MOE)
    nxt3 = jnp.concatenate([src3[1:], src3[-1:]], axis=0)
    dst3 = dst8.reshape(nblk, 1, BM_MOE)
    idx_spec = pl.BlockSpec((1, 1, BM_MOE), lambda i, be: (i, 0, 0), memory_space=pltpu.SMEM)
    grid_spec = pltpu.PrefetchScalarGridSpec(
        num_scalar_prefetch=1,
        grid=(nblk,),
        in_specs=[
            idx_spec,
            idx_spec,
            idx_spec,
            pl.BlockSpec(memory_space=pl.ANY),
            pl.BlockSpec((1, D_MODEL, D_EXPERT), lambda i, be: (be[i], 0, 0)),
            pl.BlockSpec((1, D_MODEL, D_EXPERT), lambda i, be: (be[i], 0, 0)),
            pl.BlockSpec((1, D_EXPERT, D_MODEL), lambda i, be: (be[i], 0, 0)),
        ],
        out_specs=pl.BlockSpec(memory_space=pl.ANY),
        scratch_shapes=[
            pltpu.VMEM((PARTS_MOE, HALF_MOE * SUBLANES, LANES), F32),
            pltpu.VMEM((PARTS_MOE, HALF_MOE * SUBLANES, LANES), F32),
            pltpu.VMEM((D_MODEL, D_EXPERT), BF16),
            pltpu.VMEM((D_MODEL, D_EXPERT), BF16),
            pltpu.VMEM((D_EXPERT, D_MODEL), BF16),
            pltpu.SemaphoreType.DMA((PARTS_MOE,)),
            pltpu.SemaphoreType.DMA((PARTS_MOE,)),
        ],
    )
    return pl.pallas_call(
        _moe_kernel,
        grid_spec=grid_spec,
        out_shape=jax.ShapeDtypeStruct((n_y_tiles * SUBLANES, LANES), F32),
        compiler_params=pltpu.CompilerParams(
            dimension_semantics=("arbitrary",), vmem_limit_bytes=VMEM_LIMIT),
        name="moe",
    )(block_e, src3, nxt3, dst3, h1t, w_gate, w_up, w_down)


def _final_kernel(h1t_ref, y0_ref, y1_ref, gw_ref, g_ref, b_ref, o_ref):
    n = o_ref.shape[0]
    gw = gw_ref[...]
    ffn = gw[:, 0:1] * _load_token_tiles(y0_ref, n) + gw[:, 1:2] * _load_token_tiles(y1_ref, n)
    o_ref[...] = _layer_norm(DEEPNORM_ALPHA * _load_token_tiles(h1t_ref, n) + ffn, g_ref[...], b_ref[...])


def _final_ln(h1t, y, gw, g, b):
    t = gw.shape[0]
    nt = t // TM_FIN
    const = lambda i: (0, 0)
    tiles = (TM_FIN * SUBLANES, LANES)
    return pl.pallas_call(
        _final_kernel,
        grid=(nt,),
        in_specs=[
            pl.BlockSpec(tiles, lambda i: (i, 0)),
            pl.BlockSpec(tiles, lambda i: (i, 0)),
            pl.BlockSpec(tiles, lambda i: (i + nt, 0)),
            pl.BlockSpec((TM_FIN, LANES), lambda i: (i, 0)),
            pl.BlockSpec((1, D_MODEL), const),
            pl.BlockSpec((1, D_MODEL), const),
        ],
        out_specs=pl.BlockSpec((TM_FIN, D_MODEL), lambda i: (i, 0)),
        out_shape=jax.ShapeDtypeStruct((t, D_MODEL), F32),
        compiler_params=pltpu.CompilerParams(
            dimension_semantics=("arbitrary",), vmem_limit_bytes=VMEM_LIMIT),
        name="final_ln",
    )(h1t, y, y, gw, g, b)


def _t5_causal_bucket(dist):
    max_exact = NUM_BUCKETS // 2
    d_f = jnp.maximum(dist, 1).astype(F32)
    large = max_exact + (jnp.log(d_f / max_exact) / math.log(MAX_DISTANCE / max_exact)
                         * (NUM_BUCKETS - max_exact)).astype(I32)
    large = jnp.minimum(large, NUM_BUCKETS - 1)
    return jnp.where(dist < max_exact, dist, large)


def _attn_bias_tables(rel_bias):
    qi = jnp.arange(ATTN_BLOCK)[:, None]
    kj = jnp.arange(2 * ATTN_BLOCK)[None, :]
    dist = qi + ATTN_BLOCK - kj
    bucket = _t5_causal_bucket(jnp.clip(dist, 0, None))
    hit = bucket[None, None] == jnp.arange(NUM_BUCKETS, dtype=I32)[None, :, None, None]
    bias = jnp.sum(jnp.where(hit, rel_bias.astype(F32).T[:, :, None, None], 0.0), axis=1)
    in_window = (dist >= 0) & (dist < WINDOW)
    normal = jnp.where(in_window[None], bias, -jnp.inf)
    first = jnp.where((in_window & (kj >= ATTN_BLOCK))[None], bias, -jnp.inf)
    return jnp.stack([normal, first], axis=0)


ASSIGN_BITS = 17


def _route_metadata(eid, counts, t):
    n_assign = 2 * t
    assert n_assign <= 1 << ASSIGN_BITS
    n_blk = n_assign // BM_MOE + N_EXPERTS
    padded = (counts + BM_MOE - 1) // BM_MOE * BM_MOE
    pends = jnp.cumsum(padded)
    n_pad = padded - counts
    key = (eid.reshape(-1) << (ASSIGN_BITS + 1)) | jnp.arange(n_assign, dtype=I32)
    j = jnp.arange(BM_MOE, dtype=I32)[None, :]
    ee = jnp.arange(N_EXPERTS, dtype=I32)[:, None]
    pad_key = jnp.where(j < n_pad[:, None], (ee << (ASSIGN_BITS + 1)) | (1 << ASSIGN_BITS) | j,
                        jnp.iinfo(jnp.int32).max)
    keys = jnp.sort(jnp.concatenate([key, pad_key.reshape(-1)]))
    real = (keys & (1 << ASSIGN_BITS)) == 0
    a = keys & ((1 << ASSIGN_BITS) - 1)
    tok = a >> 1
    r = jnp.arange(n_blk * BM_MOE, dtype=I32)
    src8 = jnp.where(real, tok, 0) * SUBLANES
    dst8 = jnp.where(real, (a & 1) * t + tok, n_assign + (r % BM_MOE)) * SUBLANES
    blk_start = jnp.arange(n_blk, dtype=I32)[:, None] * BM_MOE
    block_e = jnp.minimum(jnp.sum((pends[None, :] <= blk_start).astype(I32), axis=1), N_EXPERTS - 1)
    return block_e.astype(I32), src8.astype(I32), dst8.astype(I32), n_assign + BM_MOE


def kernel(x, ln_in_g, ln_in_b, rel_bias, w_in, b_gate, attn_sink, conv_w, conv_b, dt_bias, a_log, d_skip,
           ssm_norm_g, w_attn_out, w_ssm_out, w_out, ln1_g, ln1_b, w_group_router, w_expert_router,
           w_gate_e, w_up_e, w_down_e, ln2_g, ln2_b):
    nbatch, seq, d = x.shape
    assert seq == SEQ and d == D_MODEL
    t = nbatch * seq
    x2 = x.reshape(t, d)
    l = 0

    w = w_in[l]
    q_w, k_w, v_w, z_w, xs_w, b_w, c_w, dt_w, ga_w, gs_w = jnp.split(
        w, [1024, 1152, 1280, 3328, 5376, 5888, 6400, 6432, 7456], axis=1)
    k0, k1 = k_w[:, :HEAD_DIM], k_w[:, HEAD_DIM:]
    v0, v1 = v_w[:, :HEAD_DIM], v_w[:, HEAD_DIM:]
    w_cat = jnp.concatenate([z_w, xs_w, b_w, c_w, q_w * (HEAD_DIM ** -0.5), ga_w, gs_w,
                             k0, k0, k1, k1, v0, v0, v1, v1], axis=1).astype(BF16)
    w_dt = jnp.pad(dt_w, ((0, 0), (0, LANES - N_SSM_HEADS))).astype(BF16)
    row = lambda v: v.reshape(1, -1).astype(F32)
    pad_heads = lambda v: jnp.pad(v.astype(F32), (0, LANES - N_SSM_HEADS)).reshape(1, LANES)
    a_neg = pad_heads(-jnp.exp(a_log[l].astype(F32)))
    dskip_b = jnp.repeat(d_skip[l].astype(F32), D_INNER // N_SSM_HEADS).reshape(1, D_INNER)
    w_r = jnp.pad(jnp.concatenate([w_group_router[l], w_expert_router[l]], axis=1),
                  ((0, 0), (0, LANES - N_EXPERT_GROUPS - N_EXPERTS))).astype(BF16)
    bias2 = _attn_bias_tables(rel_bias)

    proj, dt = _ln_inproj(x2, row(ln_in_g), row(ln_in_b), w_cat, w_dt, conv_w[l].astype(F32), row(conv_b[l]),
                          row(b_gate[l]), pad_heads(dt_bias[l]))
    attn = _swa_attn(proj, bias2, attn_sink[l].astype(F32), nbatch)
    yssm = _ssd(proj, dt, a_neg, dskip_b, row(ssm_norm_g[l]), nbatch)
    h1t, eid, gw, cnt = _mix_route(attn, yssm, proj, x2, row(ln_in_g), row(ln_in_b),
                                   w_attn_out[l].astype(BF16), w_ssm_out[l].astype(BF16), w_out[l].astype(BF16),
                                   row(ln1_g[l]), row(ln1_b[l]), w_r)

    counts = cnt[0, N_EXPERT_GROUPS:N_EXPERT_GROUPS + N_EXPERTS]
    block_e, src8, dst8, n_y_tiles = _route_metadata(eid[:, :2], counts, t)
    y = _moe(block_e, src8, dst8, h1t, w_gate_e[l], w_up_e[l], w_down_e[l], n_y_tiles)
    out = _final_ln(h1t, y, gw, row(ln2_g[l]), row(ln2_b[l]))
    return out.reshape(nbatch, seq, d)
```

```python
import functools
import math

import jax
import jax.numpy as jnp
from jax import lax
from jax.experimental import pallas as pl
from jax.experimental.pallas import tpu as pltpu

F32 = jnp.float32
BF16 = jnp.bfloat16
I32 = jnp.int32
U32 = jnp.uint32

D_MODEL = 1024
SEQ = 2048
HEAD_DIM = 64
N_Q_HEADS = 16
N_KV_HEADS = 2
ATTN_BLOCK = 128
WINDOW = 128
NUM_BUCKETS = 32
MAX_DISTANCE = 128
D_INNER = 2048
N_SSM_HEADS = 32
N_SSM_GROUPS = 4
D_STATE = 128
CONV_WIDTH = 4
CHUNK = 128
N_EXPERT_GROUPS = 4
EXPERTS_PER_GROUP = 8
N_EXPERTS = 32
D_EXPERT = 512
DEEPNORM_ALPHA = 2.0 ** 0.25
LN_EPS = 1e-5

LANES = 128
SUBLANES = 8

COL_Z = 0
COL_XS = 2048
COL_B = 4096
COL_C = 4608
COL_Q = 5120
COL_GA = 6144
COL_GS = 7168
COL_KV = 8192
N_PROJ = 8704

TM_PROJ = 512
CH_PROJ = 256
CONV_CH = D_INNER + 2 * N_SSM_GROUPS * D_STATE
RAW_SLOTS = 3
TM_MIX = 256
TM_FIN = 512
BM_MOE = 512
HALF_MOE = 128
PARTS_MOE = BM_MOE // HALF_MOE
TD_DISPATCH = 512
DMA_PRIORITIES = 2

VMEM_LIMIT = 56 * 1024 * 1024


def _layer_norm(x, g, b):
    mu = jnp.mean(x, axis=-1, keepdims=True)
    xc = x - mu
    var = jnp.mean(xc * xc, axis=-1, keepdims=True)
    return xc * lax.rsqrt(var + LN_EPS) * g + b


assert D_MODEL == SUBLANES * LANES


def _store_token_tiles(ref, val, n):
    for c in range(SUBLANES):
        ref[pl.ds(c, n, stride=SUBLANES), :] = val[:, c * LANES:(c + 1) * LANES]


def _load_token_tiles(ref, n):
    return jnp.concatenate([ref[pl.ds(c, n, stride=SUBLANES), :] for c in range(SUBLANES)], axis=1)


def _softplus(x):
    return jnp.maximum(x, 0.0) + jnp.log1p(jnp.exp(-jnp.abs(x)))


def _silu(x):
    return x * jax.nn.sigmoid(x)


def _ln_inproj_kernel(x_ref, g_ref, b_ref, w_hbm, wdt_ref, cw_ref, cb_ref, bg_ref, dtb_ref,
                      o_ref, dt_ref, w_scr, carry_scr, raw_scr, sem):
    i = pl.program_id(0)

    @pl.when(i == 0)
    def _():
        cp = pltpu.make_async_copy(w_hbm, w_scr, sem)
        cp.start()
        carry_scr[...] = jnp.zeros_like(carry_scr)
        cp.wait()

    hb = _layer_norm(x_ref[...], g_ref[...], b_ref[...]).astype(BF16)
    tm = hb.shape[0]

    seq_start = lax.rem(i, SEQ // TM_PROJ) == 0
    sub = lax.broadcasted_iota(I32, (tm // SUBLANES, SUBLANES, CH_PROJ), 1)

    def conv_silu(r, c0):
        cc = slice(c0 - COL_XS, c0 - COL_XS + CH_PROJ)
        prev = jnp.where(seq_start, 0.0, carry_scr[:, cc])
        carry_scr[:, cc] = r[tm - SUBLANES:]
        acc = cb_ref[:, cc] + cw_ref[CONV_WIDTH - 1:CONV_WIDTH, cc] * r
        r3 = jnp.concatenate([prev, r], axis=0).reshape(tm // SUBLANES + 1, SUBLANES, CH_PROJ)
        for k in range(1, CONV_WIDTH):
            rot = pltpu.roll(r3, k, axis=1)
            rk = jnp.where(sub < k, rot[:-1], rot[1:]).reshape(tm, CH_PROJ)
            acc = acc + cw_ref[CONV_WIDTH - 1 - k:CONV_WIDTH - k, cc] * rk
        return _silu(acc)

    def epilogue(r, c0):
        if c0 < COL_XS:
            return _silu(r)
        if c0 < COL_Q:
            return conv_silu(r, c0)
        if COL_GA <= c0 < COL_KV:
            return jax.nn.sigmoid(r + bg_ref[:, c0 - COL_GA:c0 - COL_GA + CH_PROJ])
        return r

    def proj(c0):
        return jnp.dot(hb, w_scr[:, c0:c0 + CH_PROJ], preferred_element_type=F32)

    heavy = list(range(COL_XS, COL_Q, CH_PROJ))
    light = [c for c in range(0, N_PROJ, CH_PROJ) if c not in heavy]
    starts = []
    while heavy or light:
        starts += heavy[:1] + light[:2]
        heavy, light = heavy[1:], light[2:]
    zero = jnp.minimum(i, 0)
    ahead = RAW_SLOTS - 1
    for n in range(ahead):
        raw_scr[zero + n] = proj(starts[n])
    for n, c0 in enumerate(starts):
        if n + ahead < len(starts):
            raw_scr[zero + (n + ahead) % RAW_SLOTS] = proj(starts[n + ahead])
        o_ref[:, c0:c0 + CH_PROJ] = epilogue(raw_scr[zero + n % RAW_SLOTS], c0).astype(BF16)
    dt_ref[...] = _softplus(jnp.dot(hb, wdt_ref[...], preferred_element_type=F32) + dtb_ref[...])


def _ln_inproj(x2, g, b, w_cat, w_dt, conv_w, conv_b, b_gate, dt_bias):
    t = x2.shape[0]
    const = lambda i: (0, 0)
    return pl.pallas_call(
        _ln_inproj_kernel,
        grid=(t // TM_PROJ,),
        in_specs=[
            pl.BlockSpec((TM_PROJ, D_MODEL), lambda i: (i, 0)),
            pl.BlockSpec((1, D_MODEL), const),
            pl.BlockSpec((1, D_MODEL), const),
            pl.BlockSpec(memory_space=pl.ANY),
            pl.BlockSpec((D_MODEL, LANES), const),
            pl.BlockSpec((CONV_WIDTH, CONV_CH), const),
            pl.BlockSpec((1, CONV_CH), const),
            pl.BlockSpec((1, 2 * D_MODEL), const),
            pl.BlockSpec((1, LANES), const),
        ],
        out_specs=[
            pl.BlockSpec((TM_PROJ, N_PROJ), lambda i: (i, 0)),
            pl.BlockSpec((TM_PROJ, LANES), lambda i: (i, 0)),
        ],
        out_shape=[
            jax.ShapeDtypeStruct((t, N_PROJ), BF16),
            jax.ShapeDtypeStruct((t, LANES), F32),
        ],
        scratch_shapes=[
            pltpu.VMEM((D_MODEL, N_PROJ), BF16),
            pltpu.VMEM((SUBLANES, CONV_CH), F32),
            pltpu.VMEM((RAW_SLOTS, TM_PROJ, CH_PROJ), F32),
            pltpu.SemaphoreType.DMA(()),
        ],
        compiler_params=pltpu.CompilerParams(
            dimension_semantics=("arbitrary",), vmem_limit_bytes=VMEM_LIMIT),
        name="ln_inproj",
    )(x2, g, b, w_cat, w_dt, conv_w, conv_b, b_gate, dt_bias)


def _attn_kernel(sink_ref, q_ref, kvc_ref, kvp_ref, bias_ref, o_ref):
    first = jnp.where(pl.program_id(1) == 0, 1, 0)
    lo = lax.broadcasted_iota(I32, (ATTN_BLOCK, LANES), 1) < HEAD_DIM
    for kv in range(N_KV_HEADS):
        ks = slice(kv * LANES, (kv + 1) * LANES)
        vs = slice((N_KV_HEADS + kv) * LANES, (N_KV_HEADS + kv + 1) * LANES)
        kcat = jnp.concatenate([kvp_ref[:, ks], kvc_ref[:, ks]], axis=0)
        vcat = jnp.concatenate([kvp_ref[:, vs], kvc_ref[:, vs]], axis=0)
        for pp in range(4):
            p = kv * 4 + pp
            qp = q_ref[:, p * LANES:(p + 1) * LANES].astype(F32)
            qm = jnp.concatenate([jnp.where(lo, qp, 0.0), jnp.where(lo, 0.0, qp)], axis=0).astype(BF16)
            s = lax.dot_general(qm, kcat, (((1,), (1,)), ((), ())), preferred_element_type=F32)
            outs = []
            for hh in range(2):
                h = 2 * p + hh
                sh = s[hh * ATTN_BLOCK:(hh + 1) * ATTN_BLOCK] + bias_ref[first, h]
                snk = sink_ref[h]
                m = jnp.maximum(jnp.max(sh, axis=-1, keepdims=True), snk)
                pe = jnp.exp(sh - m)
                den = jnp.sum(pe, axis=-1, keepdims=True) + jnp.exp(snk - m)
                o = jnp.dot(pe.astype(BF16), vcat, preferred_element_type=F32)
                outs.append(o / den)
            o_ref[:, p * LANES:(p + 1) * LANES] = jnp.where(lo, outs[0], outs[1]).astype(BF16)


def _swa_attn(proj, bias2, sink, nbatch):
    t = proj.shape[0]
    nb = SEQ // ATTN_BLOCK
    q_blk = COL_Q // D_MODEL
    kv_blk = COL_KV // 512
    return pl.pallas_call(
        _attn_kernel,
        grid=(nbatch, nb),
        in_specs=[
            pl.BlockSpec(memory_space=pltpu.SMEM),
            pl.BlockSpec((ATTN_BLOCK, D_MODEL), lambda b, i: (b * nb + i, q_blk)),
            pl.BlockSpec((ATTN_BLOCK, 512), lambda b, i: (b * nb + i, kv_blk)),
            pl.BlockSpec((ATTN_BLOCK, 512), lambda b, i: (b * nb + jnp.maximum(i - 1, 0), kv_blk)),
            pl.BlockSpec((2, N_Q_HEADS, ATTN_BLOCK, 2 * ATTN_BLOCK), lambda b, i: (0, 0, 0, 0)),
        ],
        out_specs=pl.BlockSpec((ATTN_BLOCK, D_MODEL), lambda b, i: (b * nb + i, 0)),
        out_shape=jax.ShapeDtypeStruct((t, D_MODEL), BF16),
        compiler_params=pltpu.CompilerParams(
            dimension_semantics=("arbitrary", "arbitrary"), vmem_limit_bytes=VMEM_LIMIT),
        name="swa_attn",
    )(sink, proj, proj, proj, bias2)


def _ssd_kernel(zs_ref, xs_ref, bc_ref, dt_ref, a_ref, dsk_ref, ng_ref, y_ref, state_scr):
    @pl.when(pl.program_id(1) == 0)
    def _():
        state_scr[...] = jnp.zeros_like(state_scr)

    bm = bc_ref[:, :N_SSM_GROUPS * D_STATE]
    cm = bc_ref[:, N_SSM_GROUPS * D_STATE:]

    dt = dt_ref[...]
    da = dt * a_ref[...]
    row = lax.broadcasted_iota(I32, (CHUNK, CHUNK), 0)
    col = lax.broadcasted_iota(I32, (CHUNK, CHUNK), 1)
    causal = row >= col
    tril = jnp.where(causal, 1.0, 0.0).astype(BF16)
    hi = da.astype(BF16)
    r1 = da - hi.astype(F32)
    mid = r1.astype(BF16)
    low = (r1 - mid.astype(F32)).astype(BF16)
    acum = (jnp.dot(tril, hi, preferred_element_type=F32) + jnp.dot(tril, mid, preferred_element_type=F32)
            + jnp.dot(tril, low, preferred_element_type=F32))
    acum_t = acum.T
    dt_t = dt.T
    dtw = dt * jnp.exp(acum[CHUNK - 1:CHUNK, :] - acum)

    def over_lanes(v, e):
        return jnp.broadcast_to(v[:, e:e + 1], (CHUNK, LANES))

    def over_rows(v, e):
        return jnp.broadcast_to(v[e:e + 1, :], (CHUNK, LANES))

    lo = lax.broadcasted_iota(I32, (CHUNK, LANES), 1) < (LANES // 2)
    gw = D_INNER // N_SSM_GROUPS
    pairs = gw // LANES
    for g in range(N_SSM_GROUPS):
        c_g = cm[:, g * D_STATE:(g + 1) * D_STATE]
        b_g = bm[:, g * D_STATE:(g + 1) * D_STATE]
        cb = lax.dot_general(c_g, b_g, (((1,), (1,)), ((), ())), preferred_element_type=F32)
        cb = jnp.where(causal, cb, 0.0)
        st_g = state_scr[:, g * gw:(g + 1) * gw]
        y_off = jnp.dot(c_g, st_g.astype(BF16), preferred_element_type=F32)
        hg, xw, ea_end = [], [], []
        for jj in range(pairs):
            j = g * pairs + jj
            cs = slice(j * LANES, (j + 1) * LANES)
            xp_bf = xs_ref[:, cs]
            xp = xp_bf.astype(F32)
            ms, ea, wl = [], [], []
            for e in (2 * j, 2 * j + 1):
                a_col = over_lanes(acum, e)
                dec = jnp.exp(jnp.minimum(a_col - over_rows(acum_t, e), 0.0))
                ms.append((cb * dec * over_rows(dt_t, e)).astype(BF16))
                ea.append(jnp.exp(a_col))
                wl.append(over_lanes(dtw, e))
            zero = jnp.zeros_like(xp_bf)
            rhs = jnp.concatenate([jnp.where(lo, xp_bf, zero), jnp.where(lo, zero, xp_bf)], axis=0)
            y_diag = jnp.dot(jnp.concatenate(ms, axis=1), rhs, preferred_element_type=F32)
            ea_p = jnp.where(lo, ea[0], ea[1])
            y_p = y_diag + y_off[:, jj * LANES:(jj + 1) * LANES] * ea_p + dsk_ref[:, cs] * xp
            hg.append(y_p * zs_ref[:, cs].astype(F32))
            xw.append((xp * jnp.where(lo, wl[0], wl[1])).astype(BF16))
            ea_end.append(ea_p[CHUNK - 1:CHUNK, :])
        b_t = b_g.astype(F32).T.astype(BF16)
        upd = jnp.dot(b_t, jnp.concatenate(xw, axis=1), preferred_element_type=F32)
        state_scr[:, g * gw:(g + 1) * gw] = st_g * jnp.concatenate(ea_end, axis=1) + upd
        h_g = jnp.concatenate(hg, axis=1)
        ms_g = jnp.mean(h_g * h_g, axis=-1, keepdims=True)
        y_ref[:, g * gw:(g + 1) * gw] = (h_g * lax.rsqrt(ms_g + LN_EPS) * ng_ref[:, g * gw:(g + 1) * gw]).astype(BF16)


def _ssd(proj, dt, a_neg, dskip_b, norm_g, nbatch):
    t = proj.shape[0]
    nc = SEQ // CHUNK
    const = lambda b, c: (0, 0)
    return pl.pallas_call(
        _ssd_kernel,
        grid=(nbatch, nc),
        in_specs=[
            pl.BlockSpec((CHUNK, D_INNER), lambda b, c: (b * nc + c, COL_Z // D_INNER)),
            pl.BlockSpec((CHUNK, D_INNER), lambda b, c: (b * nc + c, COL_XS // D_INNER)),
            pl.BlockSpec((CHUNK, 1024), lambda b, c: (b * nc + c, COL_B // 1024)),
            pl.BlockSpec((CHUNK, LANES), lambda b, c: (b * nc + c, 0)),
            pl.BlockSpec((1, LANES), const),
            pl.BlockSpec((1, D_INNER), const),
            pl.BlockSpec((1, D_INNER), const),
        ],
        out_specs=pl.BlockSpec((CHUNK, D_INNER), lambda b, c: (b * nc + c, 0)),
        out_shape=jax.ShapeDtypeStruct((t, D_INNER), BF16),
        scratch_shapes=[pltpu.VMEM((D_STATE, D_INNER), F32)],
        compiler_params=pltpu.CompilerParams(
            dimension_semantics=("arbitrary", "arbitrary"), vmem_limit_bytes=VMEM_LIMIT),
        name="ssd",
    )(proj, proj, proj, dt, a_neg, dskip_b, norm_g)


def _mix_route_kernel(attn_ref, y_ref, gate_ref, x_ref, lng_ref, lnb_ref, wao_ref, wso_ref, wo_ref,
                      l1g_ref, l1b_ref, wr_ref, h1t_ref, eid_ref, gw_ref, cnt_ref, logit_scr):
    i = pl.program_id(0)

    @pl.when(i == 0)
    def _():
        cnt_ref[...] = jnp.zeros_like(cnt_ref)
        logit_scr[...] = jnp.zeros_like(logit_scr)

    logits = logit_scr[...]
    lane = lax.broadcasted_iota(I32, (TM_MIX, LANES), 1)
    neg = -jnp.inf
    lg = jnp.where(lane < N_EXPERT_GROUPS, logits, neg)
    mg = jnp.max(lg, axis=-1, keepdims=True)
    grp = jnp.min(jnp.where(lg == mg, lane, LANES), axis=-1, keepdims=True)
    p_grp = 1.0 / jnp.sum(jnp.exp(lg - mg), axis=-1, keepdims=True)
    el = lane - N_EXPERT_GROUPS
    in_grp = (el >= 0) & (el < N_EXPERTS) & (lax.shift_right_arithmetic(el, 3) == grp)
    le = jnp.where(in_grp, logits, neg)
    m1 = jnp.max(le, axis=-1, keepdims=True)
    i1 = jnp.min(jnp.where(le == m1, lane, LANES), axis=-1, keepdims=True)
    le2 = jnp.where(lane == i1, neg, le)
    m2 = jnp.max(le2, axis=-1, keepdims=True)
    i2 = jnp.min(jnp.where(le2 == m2, lane, LANES), axis=-1, keepdims=True)
    e2 = jnp.exp(m2 - m1)
    p1 = p_grp / (1.0 + e2)
    p2 = p1 * e2
    gw_ref[...] = jnp.where(lane == 0, p1, jnp.where(lane == 1, p2, 0.0))

    pick = (lane == i1) | (lane == i2)
    row = lax.broadcasted_iota(I32, (TM_MIX, LANES), 0)
    ones = jnp.where(pick, 1.0, 0.0)
    seen = ones
    shift = 1
    while shift < TM_MIX:
        seen = seen + jnp.where(row >= shift, pltpu.roll(seen, shift, axis=0), 0.0)
        shift *= 2
    before = seen - ones + cnt_ref[...].astype(F32)
    rank1 = jnp.sum(jnp.where(lane == i1, before, 0.0), axis=-1, keepdims=True).astype(I32)
    rank2 = jnp.sum(jnp.where(lane == i2, before, 0.0), axis=-1, keepdims=True).astype(I32)
    eid_ref[...] = jnp.where(lane == 0, i1 - N_EXPERT_GROUPS,
                             jnp.where(lane == 1, i2 - N_EXPERT_GROUPS,
                                       jnp.where(lane == 2, rank1, jnp.where(lane == 3, rank2, 0))))
    picked = jnp.where(pick & (i > 0), 1, 0)
    cnt_ref[...] += jnp.sum(picked, axis=0, keepdims=True)

    ab = jnp.dot(attn_ref[...], wao_ref[...], preferred_element_type=F32)
    sb = jnp.dot(y_ref[...], wso_ref[...], preferred_element_type=F32)
    gate = gate_ref[...].astype(F32)
    mixin = (gate[:, :D_MODEL] * ab + gate[:, D_MODEL:] * sb).astype(BF16)
    mix = jnp.dot(mixin, wo_ref[...], preferred_element_type=F32)
    h = _layer_norm(x_ref[...], lng_ref[...], lnb_ref[...])
    h1 = _layer_norm(DEEPNORM_ALPHA * h + mix, l1g_ref[...], l1b_ref[...])
    _store_token_tiles(h1t_ref, h1, TM_MIX)
    logit_scr[...] = jnp.dot(h1.astype(BF16), wr_ref[...], preferred_element_type=F32)


def _mix_route(attn, yssm, proj, x2, lng, lnb, w_ao, w_so, w_o, l1g, l1b, w_r):
    t = x2.shape[0]
    nt = t // TM_MIX
    const = lambda i: (0, 0)
    row = lambda i: (jnp.minimum(i, nt - 1), 0)
    prev = lambda i: (jnp.maximum(i - 1, 0), 0)
    return pl.pallas_call(
        _mix_route_kernel,
        grid=(nt + 1,),
        in_specs=[
            pl.BlockSpec((TM_MIX, D_MODEL), row),
            pl.BlockSpec((TM_MIX, D_INNER), row),
            pl.BlockSpec((TM_MIX, 2 * D_MODEL), lambda i: (jnp.minimum(i, nt - 1), COL_GA // (2 * D_MODEL))),
            pl.BlockSpec((TM_MIX, D_MODEL), row),
            pl.BlockSpec((1, D_MODEL), const),
            pl.BlockSpec((1, D_MODEL), const),
            pl.BlockSpec((D_MODEL, D_MODEL), const),
            pl.BlockSpec((D_INNER, D_MODEL), const),
            pl.BlockSpec((D_MODEL, D_MODEL), const),
            pl.BlockSpec((1, D_MODEL), const),
            pl.BlockSpec((1, D_MODEL), const),
            pl.BlockSpec((D_MODEL, LANES), const),
        ],
        out_specs=[
            pl.BlockSpec((TM_MIX * SUBLANES, LANES), row),
            pl.BlockSpec((TM_MIX, LANES), prev),
            pl.BlockSpec((TM_MIX, LANES), prev),
            pl.BlockSpec((1, LANES), const),
        ],
        out_shape=[
            jax.ShapeDtypeStruct((t * SUBLANES, LANES), F32),
            jax.ShapeDtypeStruct((t, LANES), I32),
            jax.ShapeDtypeStruct((t, LANES), F32),
            jax.ShapeDtypeStruct((1, LANES), I32),
        ],
        scratch_shapes=[pltpu.VMEM((TM_MIX, LANES), F32)],
        compiler_params=pltpu.CompilerParams(
            dimension_semantics=("arbitrary",), vmem_limit_bytes=VMEM_LIMIT),
        name="mix_route",
    )(attn, yssm, proj, x2, lng, lnb, w_ao, w_so, w_o, l1g, l1b, w_r)


def _dispatch_kernel(dest_ref, h_ref, z_ref, xr_hbm, sem, *, n_token_steps):
    def copy_all(src_ref):
        for j in range(TD_DISPATCH):
            for k in range(2):
                r8 = pl.multiple_of(dest_ref[0, 0, 2 * j + k], SUBLANES)
                pltpu.make_async_copy(src_ref.at[pl.ds(j * SUBLANES, SUBLANES), :],
                                      xr_hbm.at[pl.ds(r8, SUBLANES), :], sem).start(priority=k)

    @pl.when(pl.program_id(0) < n_token_steps)
    def _():
        copy_all(h_ref)

    @pl.when(pl.program_id(0) >= n_token_steps)
    def _():
        copy_all(z_ref)

    for k in range(2):
        pltpu.make_async_copy(h_ref, xr_hbm.at[pl.ds(0, TD_DISPATCH * SUBLANES), :], sem).wait()


def _dispatch(dest8, h1t):
    n_rows = dest8.shape[0]
    nt = h1t.shape[0] // (TD_DISPATCH * SUBLANES)
    n_steps = n_rows // (2 * TD_DISPATCH)
    zeros = jnp.zeros((TD_DISPATCH * SUBLANES, LANES), F32)
    return pl.pallas_call(
        functools.partial(_dispatch_kernel, n_token_steps=nt),
        grid=(n_steps,),
        in_specs=[
            pl.BlockSpec((1, 1, 2 * TD_DISPATCH), lambda i: (i, 0, 0), memory_space=pltpu.SMEM),
            pl.BlockSpec((TD_DISPATCH * SUBLANES, LANES), lambda i: (jnp.minimum(i, nt - 1), 0)),
            pl.BlockSpec((TD_DISPATCH * SUBLANES, LANES), lambda i: (0, 0)),
        ],
        out_specs=pl.BlockSpec(memory_space=pl.ANY),
        out_shape=jax.ShapeDtypeStruct((n_rows * SUBLANES, LANES), F32),
        scratch_shapes=[pltpu.SemaphoreType.DMA(())],
        compiler_params=pltpu.CompilerParams(
            dimension_semantics=("arbitrary",), vmem_limit_bytes=VMEM_LIMIT),
        name="dispatch",
    )(dest8.reshape(n_steps, 1, 2 * TD_DISPATCH), h1t, zeros)


def _moe_kernel(be_ref, dst_ref, x_ref, wg_ref, wu_ref, wd_ref, y_hbm, ybuf, wg_bf, wu_bf, wd_bf, ssem):
    i = pl.program_id(0)
    nblk = pl.num_programs(0)
    part_rows = HALF_MOE * SUBLANES

    def issue_scatter(s):
        for j in range(HALF_MOE):
            r8 = pl.multiple_of(dst_ref[0, 0, s * HALF_MOE + j], SUBLANES)
            pltpu.make_async_copy(ybuf.at[s, pl.ds(j * SUBLANES, SUBLANES), :], y_hbm.at[pl.ds(r8, SUBLANES), :],
                                  ssem.at[s]).start(priority=j % DMA_PRIORITIES)

    def wait_scatter(s):
        pltpu.make_async_copy(ybuf.at[s], y_hbm.at[pl.ds(0, part_rows), :], ssem.at[s]).wait()

    @pl.when((i == 0) | (be_ref[i] != be_ref[jnp.maximum(i - 1, 0)]))
    def _():
        wg_bf[...] = wg_ref[0].astype(BF16)
        wu_bf[...] = wu_ref[0].astype(BF16)
        wd_bf[...] = wd_ref[0].astype(BF16)

    for s in range(PARTS_MOE):
        @pl.when(i > 0)
        def _():
            wait_scatter(s)

        x = _load_token_tiles(x_ref.at[pl.ds(s * part_rows, part_rows), :], HALF_MOE).astype(BF16)
        gt = jnp.dot(x, wg_bf[...], preferred_element_type=F32)
        up = jnp.dot(x, wu_bf[...], preferred_element_type=F32)
        hid = (gt * jax.nn.sigmoid(gt) * up).astype(BF16)
        y = jnp.dot(hid, wd_bf[...], preferred_element_type=F32)
        _store_token_tiles(ybuf.at[s], y, HALF_MOE)
        issue_scatter(s)

    @pl.when(i == nblk - 1)
    def _():
        for s in range(PARTS_MOE):
            wait_scatter(s)


def _moe(block_e, dst8, xr, w_gate, w_up, w_down, n_y_tiles):
    nblk = dst8.shape[0] // BM_MOE
    grid_spec = pltpu.PrefetchScalarGridSpec(
        num_scalar_prefetch=1,
        grid=(nblk,),
        in_specs=[
            pl.BlockSpec((1, 1, BM_MOE), lambda i, be: (i, 0, 0), memory_space=pltpu.SMEM),
            pl.BlockSpec((BM_MOE * SUBLANES, LANES), lambda i, be: (i, 0)),
            pl.BlockSpec((1, D_MODEL, D_EXPERT), lambda i, be: (be[i], 0, 0)),
            pl.BlockSpec((1, D_MODEL, D_EXPERT), lambda i, be: (be[i], 0, 0)),
            pl.BlockSpec((1, D_EXPERT, D_MODEL), lambda i, be: (be[i], 0, 0)),
        ],
        out_specs=pl.BlockSpec(memory_space=pl.ANY),
        scratch_shapes=[
            pltpu.VMEM((PARTS_MOE, HALF_MOE * SUBLANES, LANES), F32),
            pltpu.VMEM((D_MODEL, D_EXPERT), BF16),
            pltpu.VMEM((D_MODEL, D_EXPERT), BF16),
            pltpu.VMEM((D_EXPERT, D_MODEL), BF16),
            pltpu.SemaphoreType.DMA((PARTS_MOE,)),
        ],
    )
    return pl.pallas_call(
        _moe_kernel,
        grid_spec=grid_spec,
        out_shape=jax.ShapeDtypeStruct((n_y_tiles * SUBLANES, LANES), F32),
        compiler_params=pltpu.CompilerParams(
            dimension_semantics=("arbitrary",), vmem_limit_bytes=VMEM_LIMIT),
        name="moe",
    )(block_e, dst8.reshape(nblk, 1, BM_MOE), xr, w_gate, w_up, w_down)


def _final_kernel(h1t_ref, y0_ref, y1_ref, gw_ref, g_ref, b_ref, o_ref):
    n = o_ref.shape[0]
    gw = gw_ref[...]
    ffn = gw[:, 0:1] * _load_token_tiles(y0_ref, n) + gw[:, 1:2] * _load_token_tiles(y1_ref, n)
    o_ref[...] = _layer_norm(DEEPNORM_ALPHA * _load_token_tiles(h1t_ref, n) + ffn, g_ref[...], b_ref[...])


def _final_ln(h1t, y, gw, g, b):
    t = gw.shape[0]
    nt = t // TM_FIN
    const = lambda i: (0, 0)
    tiles = (TM_FIN * SUBLANES, LANES)
    return pl.pallas_call(
        _final_kernel,
        grid=(nt,),
        in_specs=[
            pl.BlockSpec(tiles, lambda i: (i, 0)),
            pl.BlockSpec(tiles, lambda i: (i, 0)),
            pl.BlockSpec(tiles, lambda i: (i + nt, 0)),
            pl.BlockSpec((TM_FIN, LANES), lambda i: (i, 0)),
            pl.BlockSpec((1, D_MODEL), const),
            pl.BlockSpec((1, D_MODEL), const),
        ],
        out_specs=pl.BlockSpec((TM_FIN, D_MODEL), lambda i: (i, 0)),
        out_shape=jax.ShapeDtypeStruct((t, D_MODEL), F32),
        compiler_params=pltpu.CompilerParams(
            dimension_semantics=("arbitrary",), vmem_limit_bytes=VMEM_LIMIT),
        name="final_ln",
    )(h1t, y, y, gw, g, b)


def _t5_causal_bucket(dist):
    max_exact = NUM_BUCKETS // 2
    d_f = jnp.maximum(dist, 1).astype(F32)
    large = max_exact + (jnp.log(d_f / max_exact) / math.log(MAX_DISTANCE / max_exact)
                         * (NUM_BUCKETS - max_exact)).astype(I32)
    large = jnp.minimum(large, NUM_BUCKETS - 1)
    return jnp.where(dist < max_exact, dist, large)


def _attn_bias_tables(rel_bias):
    qi = jnp.arange(ATTN_BLOCK)[:, None]
    kj = jnp.arange(2 * ATTN_BLOCK)[None, :]
    dist = qi + ATTN_BLOCK - kj
    bucket = _t5_causal_bucket(jnp.clip(dist, 0, None))
    hit = bucket[None, None] == jnp.arange(NUM_BUCKETS, dtype=I32)[None, :, None, None]
    bias = jnp.sum(jnp.where(hit, rel_bias.astype(F32).T[:, :, None, None], 0.0), axis=1)
    in_window = (dist >= 0) & (dist < WINDOW)
    normal = jnp.where(in_window[None], bias, -jnp.inf)
    first = jnp.where((in_window & (kj >= ATTN_BLOCK))[None], bias, -jnp.inf)
    return jnp.stack([normal, first], axis=0)


ASSIGN_BITS = 17


def _route_metadata(eid, rank, counts, t):
    n_assign = 2 * t
    assert n_assign <= 1 << ASSIGN_BITS
    n_blk = n_assign // BM_MOE + N_EXPERTS
    padded = (counts + BM_MOE - 1) // BM_MOE * BM_MOE
    pends = jnp.cumsum(padded)
    n_pad = padded - counts
    experts = jnp.arange(N_EXPERTS, dtype=I32)
    start_of = jnp.sum(jnp.where(eid[..., None] == experts, pends - padded, 0), axis=-1)
    j = jnp.arange(BM_MOE, dtype=I32)[None, :]
    ee = jnp.arange(N_EXPERTS, dtype=I32)[:, None]
    is_tail = (j < n_pad[:, None]).reshape(-1)
    tail_row = ((pends - n_pad)[:, None] + j).reshape(-1)
    spare_row = pends[-1] + jnp.cumsum(jnp.where(is_tail, 0, 1)) - 1
    pad_rows = jnp.where(is_tail, tail_row, spare_row)
    dest8 = (jnp.concatenate([(start_of + rank).reshape(-1), pad_rows]) * SUBLANES).astype(I32)
    key = (eid.reshape(-1) << (ASSIGN_BITS + 1)) | jnp.arange(n_assign, dtype=I32)
    pad_key = jnp.where(j < n_pad[:, None], (ee << (ASSIGN_BITS + 1)) | (1 << ASSIGN_BITS) | j,
                        jnp.iinfo(jnp.int32).max)
    keys = jnp.sort(jnp.concatenate([key, pad_key.reshape(-1)]))
    real = (keys & (1 << ASSIGN_BITS)) == 0
    a = keys & ((1 << ASSIGN_BITS) - 1)
    tok = a >> 1
    r = jnp.arange(n_blk * BM_MOE, dtype=I32)
    dst8 = jnp.where(real, (a & 1) * t + tok, n_assign + (r % BM_MOE)) * SUBLANES
    blk_start = jnp.arange(n_blk, dtype=I32)[:, None] * BM_MOE
    block_e = jnp.minimum(jnp.sum((pends[None, :] <= blk_start).astype(I32), axis=1), N_EXPERTS - 1)
    return block_e.astype(I32), dest8, dst8.astype(I32), n_assign + BM_MOE


def kernel(x, ln_in_g, ln_in_b, rel_bias, w_in, b_gate, attn_sink, conv_w, conv_b, dt_bias, a_log, d_skip,
           ssm_norm_g, w_attn_out, w_ssm_out, w_out, ln1_g, ln1_b, w_group_router, w_expert_router,
           w_gate_e, w_up_e, w_down_e, ln2_g, ln2_b):
    nbatch, seq, d = x.shape
    assert seq == SEQ and d == D_MODEL
    t = nbatch * seq
    x2 = x.reshape(t, d)
    l = 0

    w = w_in[l]
    q_w, k_w, v_w, z_w, xs_w, b_w, c_w, dt_w, ga_w, gs_w = jnp.split(
        w, [1024, 1152, 1280, 3328, 5376, 5888, 6400, 6432, 7456], axis=1)
    k0, k1 = k_w[:, :HEAD_DIM], k_w[:, HEAD_DIM:]
    v0, v1 = v_w[:, :HEAD_DIM], v_w[:, HEAD_DIM:]
    w_cat = jnp.concatenate([z_w, xs_w, b_w, c_w, q_w * (HEAD_DIM ** -0.5), ga_w, gs_w,
                             k0, k0, k1, k1, v0, v0, v1, v1], axis=1).astype(BF16)
    w_dt = jnp.pad(dt_w, ((0, 0), (0, LANES - N_SSM_HEADS))).astype(BF16)
    row = lambda v: v.reshape(1, -1).astype(F32)
    pad_heads = lambda v: jnp.pad(v.astype(F32), (0, LANES - N_SSM_HEADS)).reshape(1, LANES)
    a_neg = pad_heads(-jnp.exp(a_log[l].astype(F32)))
    dskip_b = jnp.repeat(d_skip[l].astype(F32), D_INNER // N_SSM_HEADS).reshape(1, D_INNER)
    w_r = jnp.pad(jnp.concatenate([w_group_router[l], w_expert_router[l]], axis=1),
                  ((0, 0), (0, LANES - N_EXPERT_GROUPS - N_EXPERTS))).astype(BF16)
    bias2 = _attn_bias_tables(rel_bias)

    proj, dt = _ln_inproj(x2, row(ln_in_g), row(ln_in_b), w_cat, w_dt, conv_w[l].astype(F32), row(conv_b[l]),
                          row(b_gate[l]), pad_heads(dt_bias[l]))
    attn = _swa_attn(proj, bias2, attn_sink[l].astype(F32), nbatch)
    yssm = _ssd(proj, dt, a_neg, dskip_b, row(ssm_norm_g[l]), nbatch)
    h1t, eid, gw, cnt = _mix_route(attn, yssm, proj, x2, row(ln_in_g), row(ln_in_b),
                                   w_attn_out[l].astype(BF16), w_ssm_out[l].astype(BF16), w_out[l].astype(BF16),
                                   row(ln1_g[l]), row(ln1_b[l]), w_r)

    counts = cnt[0, N_EXPERT_GROUPS:N_EXPERT_GROUPS + N_EXPERTS]
    block_e, dest8, dst8, n_y_tiles = _route_metadata(eid[:, :2], eid[:, 2:4], counts, t)
    xr = _dispatch(dest8, h1t)
    y = _moe(block_e, dst8, xr, w_gate_e[l], w_up_e[l], w_down_e[l], n_y_tiles)
    out = _final_ln(h1t, y, gw, row(ln2_g[l]), row(ln2_b[l]))
    return out.reshape(nbatch, seq, d)
```

```python
import functools
import math

import jax
import jax.numpy as jnp
from jax import lax
from jax.experimental import pallas as pl
from jax.experimental.pallas import tpu as pltpu

F32 = jnp.float32
BF16 = jnp.bfloat16
I32 = jnp.int32
U32 = jnp.uint32

D_MODEL = 1024
SEQ = 2048
HEAD_DIM = 64
N_Q_HEADS = 16
N_KV_HEADS = 2
ATTN_BLOCK = 128
WINDOW = 128
NUM_BUCKETS = 32
MAX_DISTANCE = 128
D_INNER = 2048
N_SSM_HEADS = 32
N_SSM_GROUPS = 4
D_STATE = 128
CONV_WIDTH = 4
CHUNK = 128
N_EXPERT_GROUPS = 4
EXPERTS_PER_GROUP = 8
N_EXPERTS = 32
D_EXPERT = 512
DEEPNORM_ALPHA = 2.0 ** 0.25
LN_EPS = 1e-5

LANES = 128
SUBLANES = 8

COL_Z = 0
COL_XS = 2048
COL_B = 4096
COL_C = 4608
COL_Q = 5120
COL_GA = 6144
COL_GS = 7168
COL_KV = 8192
N_PROJ = 8704

TM_PROJ = 512
CH_PROJ = 256
CONV_CH = D_INNER + 2 * N_SSM_GROUPS * D_STATE
RAW_SLOTS = 3
TM_MIX = 256
TM_FIN = 512
BM_MOE = 512
HALF_MOE = 256
PARTS_MOE = BM_MOE // HALF_MOE
TD_DISPATCH = 512
DMA_PRIORITIES = 2

VMEM_LIMIT = 56 * 1024 * 1024


def _layer_norm(x, g, b):
    mu = jnp.mean(x, axis=-1, keepdims=True)
    xc = x - mu
    var = jnp.mean(xc * xc, axis=-1, keepdims=True)
    return xc * lax.rsqrt(var + LN_EPS) * g + b


assert D_MODEL == SUBLANES * LANES


def _store_token_tiles(ref, val, n):
    for c in range(SUBLANES):
        ref[pl.ds(c, n, stride=SUBLANES), :] = val[:, c * LANES:(c + 1) * LANES]


def _load_token_tiles(ref, n):
    return jnp.concatenate([ref[pl.ds(c, n, stride=SUBLANES), :] for c in range(SUBLANES)], axis=1)


def _softplus(x):
    return jnp.maximum(x, 0.0) + jnp.log1p(jnp.exp(-jnp.abs(x)))


def _silu(x):
    return x * jax.nn.sigmoid(x)


def _ln_inproj_kernel(x_ref, g_ref, b_ref, w_hbm, wdt_ref, cw_ref, cb_ref, bg_ref, dtb_ref,
                      o_ref, dt_ref, w_scr, carry_scr, raw_scr, sem):
    i = pl.program_id(0)

    @pl.when(i == 0)
    def _():
        cp = pltpu.make_async_copy(w_hbm, w_scr, sem)
        cp.start()
        carry_scr[...] = jnp.zeros_like(carry_scr)
        cp.wait()

    hb = _layer_norm(x_ref[...], g_ref[...], b_ref[...]).astype(BF16)
    tm = hb.shape[0]

    seq_start = lax.rem(i, SEQ // TM_PROJ) == 0
    sub = lax.broadcasted_iota(I32, (tm // SUBLANES, SUBLANES, CH_PROJ), 1)

    def conv_silu(r, c0):
        cc = slice(c0 - COL_XS, c0 - COL_XS + CH_PROJ)
        prev = jnp.where(seq_start, 0.0, carry_scr[:, cc])
        carry_scr[:, cc] = r[tm - SUBLANES:]
        acc = cb_ref[:, cc] + cw_ref[CONV_WIDTH - 1:CONV_WIDTH, cc] * r
        r3 = jnp.concatenate([prev, r], axis=0).reshape(tm // SUBLANES + 1, SUBLANES, CH_PROJ)
        for k in range(1, CONV_WIDTH):
            rot = pltpu.roll(r3, k, axis=1)
            rk = jnp.where(sub < k, rot[:-1], rot[1:]).reshape(tm, CH_PROJ)
            acc = acc + cw_ref[CONV_WIDTH - 1 - k:CONV_WIDTH - k, cc] * rk
        return _silu(acc)

    def epilogue(r, c0):
        if c0 < COL_XS:
            return _silu(r)
        if c0 < COL_Q:
            return conv_silu(r, c0)
        if COL_GA <= c0 < COL_KV:
            return jax.nn.sigmoid(r + bg_ref[:, c0 - COL_GA:c0 - COL_GA + CH_PROJ])
        return r

    def proj(c0):
        return jnp.dot(hb, w_scr[:, c0:c0 + CH_PROJ], preferred_element_type=F32)

    heavy = list(range(COL_XS, COL_Q, CH_PROJ))
    light = [c for c in range(0, N_PROJ, CH_PROJ) if c not in heavy]
    starts = []
    while heavy or light:
        starts += heavy[:1] + light[:2]
        heavy, light = heavy[1:], light[2:]
    zero = jnp.minimum(i, 0)
    ahead = RAW_SLOTS - 1
    for n in range(ahead):
        raw_scr[zero + n] = proj(starts[n])
    for n, c0 in enumerate(starts):
        if n + ahead < len(starts):
            raw_scr[zero + (n + ahead) % RAW_SLOTS] = proj(starts[n + ahead])
        o_ref[:, c0:c0 + CH_PROJ] = epilogue(raw_scr[zero + n % RAW_SLOTS], c0).astype(BF16)
    dt_ref[...] = _softplus(jnp.dot(hb, wdt_ref[...], preferred_element_type=F32) + dtb_ref[...])


def _ln_inproj(x2, g, b, w_cat, w_dt, conv_w, conv_b, b_gate, dt_bias):
    t = x2.shape[0]
    const = lambda i: (0, 0)
    return pl.pallas_call(
        _ln_inproj_kernel,
        grid=(t // TM_PROJ,),
        in_specs=[
            pl.BlockSpec((TM_PROJ, D_MODEL), lambda i: (i, 0)),
            pl.BlockSpec((1, D_MODEL), const),
            pl.BlockSpec((1, D_MODEL), const),
            pl.BlockSpec(memory_space=pl.ANY),
            pl.BlockSpec((D_MODEL, LANES), const),
            pl.BlockSpec((CONV_WIDTH, CONV_CH), const),
            pl.BlockSpec((1, CONV_CH), const),
            pl.BlockSpec((1, 2 * D_MODEL), const),
            pl.BlockSpec((1, LANES), const),
        ],
        out_specs=[
            pl.BlockSpec((TM_PROJ, N_PROJ), lambda i: (i, 0)),
            pl.BlockSpec((TM_PROJ, LANES), lambda i: (i, 0)),
        ],
        out_shape=[
            jax.ShapeDtypeStruct((t, N_PROJ), BF16),
            jax.ShapeDtypeStruct((t, LANES), F32),
        ],
        scratch_shapes=[
            pltpu.VMEM((D_MODEL, N_PROJ), BF16),
            pltpu.VMEM((SUBLANES, CONV_CH), F32),
            pltpu.VMEM((RAW_SLOTS, TM_PROJ, CH_PROJ), F32),
            pltpu.SemaphoreType.DMA(()),
        ],
        compiler_params=pltpu.CompilerParams(
            dimension_semantics=("arbitrary",), vmem_limit_bytes=VMEM_LIMIT),
        name="ln_inproj",
    )(x2, g, b, w_cat, w_dt, conv_w, conv_b, b_gate, dt_bias)


def _attn_kernel(sink_ref, q_ref, kvc_ref, kvp_ref, bias_ref, o_ref):
    first = jnp.where(pl.program_id(1) == 0, 1, 0)
    lo = lax.broadcasted_iota(I32, (ATTN_BLOCK, LANES), 1) < HEAD_DIM
    for kv in range(N_KV_HEADS):
        ks = slice(kv * LANES, (kv + 1) * LANES)
        vs = slice((N_KV_HEADS + kv) * LANES, (N_KV_HEADS + kv + 1) * LANES)
        kcat = jnp.concatenate([kvp_ref[:, ks], kvc_ref[:, ks]], axis=0)
        vcat = jnp.concatenate([kvp_ref[:, vs], kvc_ref[:, vs]], axis=0)
        for pp in range(4):
            p = kv * 4 + pp
            qp = q_ref[:, p * LANES:(p + 1) * LANES].astype(F32)
            qm = jnp.concatenate([jnp.where(lo, qp, 0.0), jnp.where(lo, 0.0, qp)], axis=0).astype(BF16)
            s = lax.dot_general(qm, kcat, (((1,), (1,)), ((), ())), preferred_element_type=F32)
            outs = []
            for hh in range(2):
                h = 2 * p + hh
                sh = s[hh * ATTN_BLOCK:(hh + 1) * ATTN_BLOCK] + bias_ref[first, h]
                snk = sink_ref[h]
                m = jnp.maximum(jnp.max(sh, axis=-1, keepdims=True), snk)
                pe = jnp.exp(sh - m)
                den = jnp.sum(pe, axis=-1, keepdims=True) + jnp.exp(snk - m)
                o = jnp.dot(pe.astype(BF16), vcat, preferred_element_type=F32)
                outs.append(o / den)
            o_ref[:, p * LANES:(p + 1) * LANES] = jnp.where(lo, outs[0], outs[1]).astype(BF16)


def _swa_attn(proj, bias2, sink, nbatch):
    t = proj.shape[0]
    nb = SEQ // ATTN_BLOCK
    q_blk = COL_Q // D_MODEL
    kv_blk = COL_KV // 512
    return pl.pallas_call(
        _attn_kernel,
        grid=(nbatch, nb),
        in_specs=[
            pl.BlockSpec(memory_space=pltpu.SMEM),
            pl.BlockSpec((ATTN_BLOCK, D_MODEL), lambda b, i: (b * nb + i, q_blk)),
            pl.BlockSpec((ATTN_BLOCK, 512), lambda b, i: (b * nb + i, kv_blk)),
            pl.BlockSpec((ATTN_BLOCK, 512), lambda b, i: (b * nb + jnp.maximum(i - 1, 0), kv_blk)),
            pl.BlockSpec((2, N_Q_HEADS, ATTN_BLOCK, 2 * ATTN_BLOCK), lambda b, i: (0, 0, 0, 0)),
        ],
        out_specs=pl.BlockSpec((ATTN_BLOCK, D_MODEL), lambda b, i: (b * nb + i, 0)),
        out_shape=jax.ShapeDtypeStruct((t, D_MODEL), BF16),
        compiler_params=pltpu.CompilerParams(
            dimension_semantics=("arbitrary", "arbitrary"), vmem_limit_bytes=VMEM_LIMIT),
        name="swa_attn",
    )(sink, proj, proj, proj, bias2)


def _ssd_kernel(zs_ref, xs_ref, bc_ref, dt_ref, a_ref, dsk_ref, ng_ref, y_ref, state_scr):
    @pl.when(pl.program_id(1) == 0)
    def _():
        state_scr[...] = jnp.zeros_like(state_scr)

    bm = bc_ref[:, :N_SSM_GROUPS * D_STATE]
    cm = bc_ref[:, N_SSM_GROUPS * D_STATE:]

    dt = dt_ref[...]
    da = dt * a_ref[...]
    row = lax.broadcasted_iota(I32, (CHUNK, CHUNK), 0)
    col = lax.broadcasted_iota(I32, (CHUNK, CHUNK), 1)
    causal = row >= col
    tril = jnp.where(causal, 1.0, 0.0).astype(BF16)
    hi = da.astype(BF16)
    r1 = da - hi.astype(F32)
    mid = r1.astype(BF16)
    low = (r1 - mid.astype(F32)).astype(BF16)
    acum = (jnp.dot(tril, hi, preferred_element_type=F32) + jnp.dot(tril, mid, preferred_element_type=F32)
            + jnp.dot(tril, low, preferred_element_type=F32))
    acum_t = acum.T
    dt_t = dt.T
    dtw = dt * jnp.exp(acum[CHUNK - 1:CHUNK, :] - acum)

    def over_lanes(v, e):
        return jnp.broadcast_to(v[:, e:e + 1], (CHUNK, LANES))

    def over_rows(v, e):
        return jnp.broadcast_to(v[e:e + 1, :], (CHUNK, LANES))

    lo = lax.broadcasted_iota(I32, (CHUNK, LANES), 1) < (LANES // 2)
    gw = D_INNER // N_SSM_GROUPS
    pairs = gw // LANES
    for g in range(N_SSM_GROUPS):
        c_g = cm[:, g * D_STATE:(g + 1) * D_STATE]
        b_g = bm[:, g * D_STATE:(g + 1) * D_STATE]
        cb = lax.dot_general(c_g, b_g, (((1,), (1,)), ((), ())), preferred_element_type=F32)
        cb = jnp.where(causal, cb, 0.0)
        st_g = state_scr[:, g * gw:(g + 1) * gw]
        y_off = jnp.dot(c_g, st_g.astype(BF16), preferred_element_type=F32)
        hg, xw, ea_end = [], [], []
        for jj in range(pairs):
            j = g * pairs + jj
            cs = slice(j * LANES, (j + 1) * LANES)
            xp_bf = xs_ref[:, cs]
            xp = xp_bf.astype(F32)
            ms, ea, wl = [], [], []
            for e in (2 * j, 2 * j + 1):
                a_col = over_lanes(acum, e)
                dec = jnp.exp(jnp.minimum(a_col - over_rows(acum_t, e), 0.0))
                ms.append((cb * dec * over_rows(dt_t, e)).astype(BF16))
                ea.append(jnp.exp(a_col))
                wl.append(over_lanes(dtw, e))
            zero = jnp.zeros_like(xp_bf)
            rhs = jnp.concatenate([jnp.where(lo, xp_bf, zero), jnp.where(lo, zero, xp_bf)], axis=0)
            y_diag = jnp.dot(jnp.concatenate(ms, axis=1), rhs, preferred_element_type=F32)
            ea_p = jnp.where(lo, ea[0], ea[1])
            y_p = y_diag + y_off[:, jj * LANES:(jj + 1) * LANES] * ea_p + dsk_ref[:, cs] * xp
            hg.append(y_p * zs_ref[:, cs].astype(F32))
            xw.append((xp * jnp.where(lo, wl[0], wl[1])).astype(BF16))
            ea_end.append(ea_p[CHUNK - 1:CHUNK, :])
        b_t = b_g.astype(F32).T.astype(BF16)
        upd = jnp.dot(b_t, jnp.concatenate(xw, axis=1), preferred_element_type=F32)
        state_scr[:, g * gw:(g + 1) * gw] = st_g * jnp.concatenate(ea_end, axis=1) + upd
        h_g = jnp.concatenate(hg, axis=1)
        ms_g = jnp.mean(h_g * h_g, axis=-1, keepdims=True)
        y_ref[:, g * gw:(g + 1) * gw] = (h_g * lax.rsqrt(ms_g + LN_EPS) * ng_ref[:, g * gw:(g + 1) * gw]).astype(BF16)


def _ssd(proj, dt, a_neg, dskip_b, norm_g, nbatch):
    t = proj.shape[0]
    nc = SEQ // CHUNK
    const = lambda b, c: (0, 0)
    return pl.pallas_call(
        _ssd_kernel,
        grid=(nbatch, nc),
        in_specs=[
            pl.BlockSpec((CHUNK, D_INNER), lambda b, c: (b * nc + c, COL_Z // D_INNER)),
            pl.BlockSpec((CHUNK, D_INNER), lambda b, c: (b * nc + c, COL_XS // D_INNER)),
            pl.BlockSpec((CHUNK, 1024), lambda b, c: (b * nc + c, COL_B // 1024)),
            pl.BlockSpec((CHUNK, LANES), lambda b, c: (b * nc + c, 0)),
            pl.BlockSpec((1, LANES), const),
            pl.BlockSpec((1, D_INNER), const),
            pl.BlockSpec((1, D_INNER), const),
        ],
        out_specs=pl.BlockSpec((CHUNK, D_INNER), lambda b, c: (b * nc + c, 0)),
        out_shape=jax.ShapeDtypeStruct((t, D_INNER), BF16),
        scratch_shapes=[pltpu.VMEM((D_STATE, D_INNER), F32)],
        compiler_params=pltpu.CompilerParams(
            dimension_semantics=("arbitrary", "arbitrary"), vmem_limit_bytes=VMEM_LIMIT),
        name="ssd",
    )(proj, proj, proj, dt, a_neg, dskip_b, norm_g)


def _mix_route_kernel(attn_ref, y_ref, gate_ref, x_ref, lng_ref, lnb_ref, wao_ref, wso_ref, wo_ref,
                      l1g_ref, l1b_ref, wr_ref, h1t_ref, eid_ref, gw_ref, cnt_ref, mix_scr, stage_scr, logit_scr):
    i = pl.program_id(0)

    @pl.when(i == 0)
    def _():
        cnt_ref[...] = jnp.zeros_like(cnt_ref)
        mix_scr[...] = jnp.zeros_like(mix_scr)
        logit_scr[...] = jnp.zeros_like(logit_scr)

    logits = logit_scr[...]
    lane = lax.broadcasted_iota(I32, (TM_MIX, LANES), 1)
    neg = -jnp.inf
    lg = jnp.where(lane < N_EXPERT_GROUPS, logits, neg)
    mg = jnp.max(lg, axis=-1, keepdims=True)
    grp = jnp.min(jnp.where(lg == mg, lane, LANES), axis=-1, keepdims=True)
    p_grp = 1.0 / jnp.sum(jnp.exp(lg - mg), axis=-1, keepdims=True)
    el = lane - N_EXPERT_GROUPS
    in_grp = (el >= 0) & (el < N_EXPERTS) & (lax.shift_right_arithmetic(el, 3) == grp)
    le = jnp.where(in_grp, logits, neg)
    m1 = jnp.max(le, axis=-1, keepdims=True)
    i1 = jnp.min(jnp.where(le == m1, lane, LANES), axis=-1, keepdims=True)
    le2 = jnp.where(lane == i1, neg, le)
    m2 = jnp.max(le2, axis=-1, keepdims=True)
    i2 = jnp.min(jnp.where(le2 == m2, lane, LANES), axis=-1, keepdims=True)
    e2 = jnp.exp(m2 - m1)
    p1 = p_grp / (1.0 + e2)
    p2 = p1 * e2
    gw_ref[...] = jnp.where(lane == 0, p1, jnp.where(lane == 1, p2, 0.0))

    pick = (lane == i1) | (lane == i2)
    row = lax.broadcasted_iota(I32, (TM_MIX, LANES), 0)
    ones = jnp.where(pick, 1.0, 0.0)
    seen = ones
    shift = 1
    while shift < TM_MIX:
        seen = seen + jnp.where(row >= shift, pltpu.roll(seen, shift, axis=0), 0.0)
        shift *= 2
    before = seen - ones + cnt_ref[...].astype(F32)
    rank1 = jnp.sum(jnp.where(lane == i1, before, 0.0), axis=-1, keepdims=True).astype(I32)
    rank2 = jnp.sum(jnp.where(lane == i2, before, 0.0), axis=-1, keepdims=True).astype(I32)
    eid_ref[...] = jnp.where(lane == 0, i1 - N_EXPERT_GROUPS,
                             jnp.where(lane == 1, i2 - N_EXPERT_GROUPS,
                                       jnp.where(lane == 2, rank1, jnp.where(lane == 3, rank2, 0))))
    picked = jnp.where(pick & (i > 1), 1, 0)
    cnt_ref[...] += jnp.sum(picked, axis=0, keepdims=True)

    slot = lax.rem(i, 2)
    mix_prev = mix_scr[1 - slot]
    zero = jnp.minimum(i, 0)
    stage_scr[zero] = jnp.dot(attn_ref[...], wao_ref[...], preferred_element_type=F32)
    stage_scr[zero + 1] = jnp.dot(y_ref[...], wso_ref[...], preferred_element_type=F32)
    gate = gate_ref[...].astype(F32)
    mixin = (gate[:, :D_MODEL] * stage_scr[zero] + gate[:, D_MODEL:] * stage_scr[zero + 1]).astype(BF16)
    mix_scr[slot] = jnp.dot(mixin, wo_ref[...], preferred_element_type=F32)

    h = _layer_norm(x_ref[...], lng_ref[...], lnb_ref[...])
    h1 = _layer_norm(DEEPNORM_ALPHA * h + mix_prev, l1g_ref[...], l1b_ref[...])
    _store_token_tiles(h1t_ref, h1, TM_MIX)
    logit_scr[...] = jnp.dot(h1.astype(BF16), wr_ref[...], preferred_element_type=F32)


def _mix_route(attn, yssm, proj, x2, lng, lnb, w_ao, w_so, w_o, l1g, l1b, w_r):
    t = x2.shape[0]
    nt = t // TM_MIX
    const = lambda i: (0, 0)
    row = lambda i: (jnp.minimum(i, nt - 1), 0)
    prev = lambda i: (jnp.clip(i - 1, 0, nt - 1), 0)
    prev2 = lambda i: (jnp.maximum(i - 2, 0), 0)
    return pl.pallas_call(
        _mix_route_kernel,
        grid=(nt + 2,),
        in_specs=[
            pl.BlockSpec((TM_MIX, D_MODEL), row),
            pl.BlockSpec((TM_MIX, D_INNER), row),
            pl.BlockSpec((TM_MIX, 2 * D_MODEL), lambda i: (jnp.minimum(i, nt - 1), COL_GA // (2 * D_MODEL))),
            pl.BlockSpec((TM_MIX, D_MODEL), prev),
            pl.BlockSpec((1, D_MODEL), const),
            pl.BlockSpec((1, D_MODEL), const),
            pl.BlockSpec((D_MODEL, D_MODEL), const),
            pl.BlockSpec((D_INNER, D_MODEL), const),
            pl.BlockSpec((D_MODEL, D_MODEL), const),
            pl.BlockSpec((1, D_MODEL), const),
            pl.BlockSpec((1, D_MODEL), const),
            pl.BlockSpec((D_MODEL, LANES), const),
        ],
        out_specs=[
            pl.BlockSpec((TM_MIX * SUBLANES, LANES), prev),
            pl.BlockSpec((TM_MIX, LANES), prev2),
            pl.BlockSpec((TM_MIX, LANES), prev2),
            pl.BlockSpec((1, LANES), const),
        ],
        out_shape=[
            jax.ShapeDtypeStruct((t * SUBLANES, LANES), F32),
            jax.ShapeDtypeStruct((t, LANES), I32),
            jax.ShapeDtypeStruct((t, LANES), F32),
            jax.ShapeDtypeStruct((1, LANES), I32),
        ],
        scratch_shapes=[pltpu.VMEM((2, TM_MIX, D_MODEL), F32), pltpu.VMEM((2, TM_MIX, D_MODEL), F32),
                        pltpu.VMEM((TM_MIX, LANES), F32)],
        compiler_params=pltpu.CompilerParams(
            dimension_semantics=("arbitrary",), vmem_limit_bytes=VMEM_LIMIT),
        name="mix_route",
    )(attn, yssm, proj, x2, lng, lnb, w_ao, w_so, w_o, l1g, l1b, w_r)


def _dispatch_kernel(dest_ref, h_ref, z_ref, xr_hbm, sem, *, n_token_steps):
    def copy_all(src_ref):
        for j in range(TD_DISPATCH):
            for k in range(2):
                r8 = pl.multiple_of(dest_ref[0, 0, 2 * j + k], SUBLANES)
                pltpu.make_async_copy(src_ref.at[pl.ds(j * SUBLANES, SUBLANES), :],
                                      xr_hbm.at[pl.ds(r8, SUBLANES), :], sem).start(priority=k)

    @pl.when(pl.program_id(0) < n_token_steps)
    def _():
        copy_all(h_ref)

    @pl.when(pl.program_id(0) >= n_token_steps)
    def _():
        copy_all(z_ref)

    for k in range(2):
        pltpu.make_async_copy(h_ref, xr_hbm.at[pl.ds(0, TD_DISPATCH * SUBLANES), :], sem).wait()


def _dispatch(dest8, h1t):
    n_rows = dest8.shape[0]
    nt = h1t.shape[0] // (TD_DISPATCH * SUBLANES)
    n_steps = n_rows // (2 * TD_DISPATCH)
    zeros = jnp.zeros((TD_DISPATCH * SUBLANES, LANES), F32)
    return pl.pallas_call(
        functools.partial(_dispatch_kernel, n_token_steps=nt),
        grid=(n_steps,),
        in_specs=[
            pl.BlockSpec((1, 1, 2 * TD_DISPATCH), lambda i: (i, 0, 0), memory_space=pltpu.SMEM),
            pl.BlockSpec((TD_DISPATCH * SUBLANES, LANES), lambda i: (jnp.minimum(i, nt - 1), 0)),
            pl.BlockSpec((TD_DISPATCH * SUBLANES, LANES), lambda i: (0, 0)),
        ],
        out_specs=pl.BlockSpec(memory_space=pl.ANY),
        out_shape=jax.ShapeDtypeStruct((n_rows * SUBLANES, LANES), F32),
        scratch_shapes=[pltpu.SemaphoreType.DMA(())],
        compiler_params=pltpu.CompilerParams(
            dimension_semantics=("arbitrary",), vmem_limit_bytes=VMEM_LIMIT),
        name="dispatch",
    )(dest8.reshape(n_steps, 1, 2 * TD_DISPATCH), h1t, zeros)


def _moe_kernel(be_ref, dst_ref, x_ref, wg_ref, wu_ref, wd_ref, y_hbm, ybuf, wg_bf, wu_bf, wd_bf, ssem):
    i = pl.program_id(0)
    nblk = pl.num_programs(0)
    part_rows = HALF_MOE * SUBLANES

    def issue_scatter(s):
        for j in range(HALF_MOE):
            r8 = pl.multiple_of(dst_ref[0, 0, s * HALF_MOE + j], SUBLANES)
            pltpu.make_async_copy(ybuf.at[s, pl.ds(j * SUBLANES, SUBLANES), :], y_hbm.at[pl.ds(r8, SUBLANES), :],
                                  ssem.at[s]).start(priority=j % DMA_PRIORITIES)

    def wait_scatter(s):
        pltpu.make_async_copy(ybuf.at[s], y_hbm.at[pl.ds(0, part_rows), :], ssem.at[s]).wait()

    @pl.when((i == 0) | (be_ref[i] != be_ref[jnp.maximum(i - 1, 0)]))
    def _():
        wg_bf[...] = wg_ref[0].astype(BF16)
        wu_bf[...] = wu_ref[0].astype(BF16)
        wd_bf[...] = wd_ref[0].astype(BF16)

    for s in range(PARTS_MOE):
        @pl.when(i > 0)
        def _():
            wait_scatter(s)

        x = _load_token_tiles(x_ref.at[pl.ds(s * part_rows, part_rows), :], HALF_MOE).astype(BF16)
        gt = jnp.dot(x, wg_bf[...], preferred_element_type=F32)
        up = jnp.dot(x, wu_bf[...], preferred_element_type=F32)
        hid = (gt * jax.nn.sigmoid(gt) * up).astype(BF16)
        y = jnp.dot(hid, wd_bf[...], preferred_element_type=F32)
        _store_token_tiles(ybuf.at[s], y, HALF_MOE)
        issue_scatter(s)

    @pl.when(i == nblk - 1)
    def _():
        for s in range(PARTS_MOE):
            wait_scatter(s)


def _moe(block_e, dst8, xr, w_gate, w_up, w_down, n_y_tiles):
    nblk = dst8.shape[0] // BM_MOE
    grid_spec = pltpu.PrefetchScalarGridSpec(
        num_scalar_prefetch=1,
        grid=(nblk,),
        in_specs=[
            pl.BlockSpec((1, 1, BM_MOE), lambda i, be: (i, 0, 0), memory_space=pltpu.SMEM),
            pl.BlockSpec((BM_MOE * SUBLANES, LANES), lambda i, be: (i, 0)),
            pl.BlockSpec((1, D_MODEL, D_EXPERT), lambda i, be: (be[i], 0, 0)),
            pl.BlockSpec((1, D_MODEL, D_EXPERT), lambda i, be: (be[i], 0, 0)),
            pl.BlockSpec((1, D_EXPERT, D_MODEL), lambda i, be: (be[i], 0, 0)),
        ],
        out_specs=pl.BlockSpec(memory_space=pl.ANY),
        scratch_shapes=[
            pltpu.VMEM((PARTS_MOE, HALF_MOE * SUBLANES, LANES), F32),
            pltpu.VMEM((D_MODEL, D_EXPERT), BF16),
            pltpu.VMEM((D_MODEL, D_EXPERT), BF16),
            pltpu.VMEM((D_EXPERT, D_MODEL), BF16),
            pltpu.SemaphoreType.DMA((PARTS_MOE,)),
        ],
    )
    return pl.pallas_call(
        _moe_kernel,
        grid_spec=grid_spec,
        out_shape=jax.ShapeDtypeStruct((n_y_tiles * SUBLANES, LANES), F32),
        compiler_params=pltpu.CompilerParams(
            dimension_semantics=("arbitrary",), vmem_limit_bytes=VMEM_LIMIT),
        name="moe",
    )(block_e, dst8.reshape(nblk, 1, BM_MOE), xr, w_gate, w_up, w_down)


def _final_kernel(h1t_ref, y0_ref, y1_ref, gw_ref, g_ref, b_ref, o_ref):
    n = o_ref.shape[0]
    gw = gw_ref[...]
    ffn = gw[:, 0:1] * _load_token_tiles(y0_ref, n) + gw[:, 1:2] * _load_token_tiles(y1_ref, n)
    o_ref[...] = _layer_norm(DEEPNORM_ALPHA * _load_token_tiles(h1t_ref, n) + ffn, g_ref[...], b_ref[...])


def _final_ln(h1t, y, gw, g, b):
    t = gw.shape[0]
    nt = t // TM_FIN
    const = lambda i: (0, 0)
    tiles = (TM_FIN * SUBLANES, LANES)
    return pl.pallas_call(
        _final_kernel,
        grid=(nt,),
        in_specs=[
            pl.BlockSpec(tiles, lambda i: (i, 0)),
            pl.BlockSpec(tiles, lambda i: (i, 0)),
            pl.BlockSpec(tiles, lambda i: (i + nt, 0)),
            pl.BlockSpec((TM_FIN, LANES), lambda i: (i, 0)),
            pl.BlockSpec((1, D_MODEL), const),
            pl.BlockSpec((1, D_MODEL), const),
        ],
        out_specs=pl.BlockSpec((TM_FIN, D_MODEL), lambda i: (i, 0)),
        out_shape=jax.ShapeDtypeStruct((t, D_MODEL), F32),
        compiler_params=pltpu.CompilerParams(
            dimension_semantics=("arbitrary",), vmem_limit_bytes=VMEM_LIMIT),
        name="final_ln",
    )(h1t, y, y, gw, g, b)


def _t5_causal_bucket(dist):
    max_exact = NUM_BUCKETS // 2
    d_f = jnp.maximum(dist, 1).astype(F32)
    large = max_exact + (jnp.log(d_f / max_exact) / math.log(MAX_DISTANCE / max_exact)
                         * (NUM_BUCKETS - max_exact)).astype(I32)
    large = jnp.minimum(large, NUM_BUCKETS - 1)
    return jnp.where(dist < max_exact, dist, large)


def _attn_bias_tables(rel_bias):
    qi = jnp.arange(ATTN_BLOCK)[:, None]
    kj = jnp.arange(2 * ATTN_BLOCK)[None, :]
    dist = qi + ATTN_BLOCK - kj
    bucket = _t5_causal_bucket(jnp.clip(dist, 0, None))
    hit = bucket[None, None] == jnp.arange(NUM_BUCKETS, dtype=I32)[None, :, None, None]
    bias = jnp.sum(jnp.where(hit, rel_bias.astype(F32).T[:, :, None, None], 0.0), axis=1)
    in_window = (dist >= 0) & (dist < WINDOW)
    normal = jnp.where(in_window[None], bias, -jnp.inf)
    first = jnp.where((in_window & (kj >= ATTN_BLOCK))[None], bias, -jnp.inf)
    return jnp.stack([normal, first], axis=0)


ASSIGN_BITS = 17


def _route_metadata(eid, rank, counts, t):
    n_assign = 2 * t
    assert n_assign <= 1 << ASSIGN_BITS
    n_blk = n_assign // BM_MOE + N_EXPERTS
    padded = (counts + BM_MOE - 1) // BM_MOE * BM_MOE
    pends = jnp.cumsum(padded)
    n_pad = padded - counts
    experts = jnp.arange(N_EXPERTS, dtype=I32)
    start_of = jnp.sum(jnp.where(eid[..., None] == experts, pends - padded, 0), axis=-1)
    j = jnp.arange(BM_MOE, dtype=I32)[None, :]
    ee = jnp.arange(N_EXPERTS, dtype=I32)[:, None]
    is_tail = (j < n_pad[:, None]).reshape(-1)
    tail_row = ((pends - n_pad)[:, None] + j).reshape(-1)
    spare_row = pends[-1] + jnp.cumsum(jnp.where(is_tail, 0, 1)) - 1
    pad_rows = jnp.where(is_tail, tail_row, spare_row)
    dest8 = (jnp.concatenate([(start_of + rank).reshape(-1), pad_rows]) * SUBLANES).astype(I32)
    key = (eid.reshape(-1) << (ASSIGN_BITS + 1)) | jnp.arange(n_assign, dtype=I32)
    pad_key = jnp.where(j < n_pad[:, None], (ee << (ASSIGN_BITS + 1)) | (1 << ASSIGN_BITS) | j,
                        jnp.iinfo(jnp.int32).max)
    keys = jnp.sort(jnp.concatenate([key, pad_key.reshape(-1)]))
    real = (keys & (1 << ASSIGN_BITS)) == 0
    a = keys & ((1 << ASSIGN_BITS) - 1)
    tok = a >> 1
    r = jnp.arange(n_blk * BM_MOE, dtype=I32)
    dst8 = jnp.where(real, (a & 1) * t + tok, n_assign + (r % BM_MOE)) * SUBLANES
    blk_start = jnp.arange(n_blk, dtype=I32)[:, None] * BM_MOE
    block_e = jnp.minimum(jnp.sum((pends[None, :] <= blk_start).astype(I32), axis=1), N_EXPERTS - 1)
    return block_e.astype(I32), dest8, dst8.astype(I32), n_assign + BM_MOE


def kernel(x, ln_in_g, ln_in_b, rel_bias, w_in, b_gate, attn_sink, conv_w, conv_b, dt_bias, a_log, d_skip,
           ssm_norm_g, w_attn_out, w_ssm_out, w_out, ln1_g, ln1_b, w_group_router, w_expert_router,
           w_gate_e, w_up_e, w_down_e, ln2_g, ln2_b):
    nbatch, seq, d = x.shape
    assert seq == SEQ and d == D_MODEL
    t = nbatch * seq
    x2 = x.reshape(t, d)
    l = 0

    w = w_in[l]
    q_w, k_w, v_w, z_w, xs_w, b_w, c_w, dt_w, ga_w, gs_w = jnp.split(
        w, [1024, 1152, 1280, 3328, 5376, 5888, 6400, 6432, 7456], axis=1)
    k0, k1 = k_w[:, :HEAD_DIM], k_w[:, HEAD_DIM:]
    v0, v1 = v_w[:, :HEAD_DIM], v_w[:, HEAD_DIM:]
    w_cat = jnp.concatenate([z_w, xs_w, b_w, c_w, q_w * (HEAD_DIM ** -0.5), ga_w, gs_w,
                             k0, k0, k1, k1, v0, v0, v1, v1], axis=1).astype(BF16)
    w_dt = jnp.pad(dt_w, ((0, 0), (0, LANES - N_SSM_HEADS))).astype(BF16)
    row = lambda v: v.reshape(1, -1).astype(F32)
    pad_heads = lambda v: jnp.pad(v.astype(F32), (0, LANES - N_SSM_HEADS)).reshape(1, LANES)
    a_neg = pad_heads(-jnp.exp(a_log[l].astype(F32)))
    dskip_b = jnp.repeat(d_skip[l].astype(F32), D_INNER // N_SSM_HEADS).reshape(1, D_INNER)
    w_r = jnp.pad(jnp.concatenate([w_group_router[l], w_expert_router[l]], axis=1),
                  ((0, 0), (0, LANES - N_EXPERT_GROUPS - N_EXPERTS))).astype(BF16)
    bias2 = _attn_bias_tables(rel_bias)

    proj, dt = _ln_inproj(x2, row(ln_in_g), row(ln_in_b), w_cat, w_dt, conv_w[l].astype(F32), row(conv_b[l]),
                          row(b_gate[l]), pad_heads(dt_bias[l]))
    attn = _swa_attn(proj, bias2, attn_sink[l].astype(F32), nbatch)
    yssm = _ssd(proj, dt, a_neg, dskip_b, row(ssm_norm_g[l]), nbatch)
    h1t, eid, gw, cnt = _mix_route(attn, yssm, proj, x2, row(ln_in_g), row(ln_in_b),
                                   w_attn_out[l].astype(BF16), w_ssm_out[l].astype(BF16), w_out[l].astype(BF16),
                                   row(ln1_g[l]), row(ln1_b[l]), w_r)

    counts = cnt[0, N_EXPERT_GROUPS:N_EXPERT_GROUPS + N_EXPERTS]
    block_e, dest8, dst8, n_y_tiles = _route_metadata(eid[:, :2], eid[:, 2:4], counts, t)
    xr = _dispatch(dest8, h1t)
    y = _moe(block_e, dst8, xr, w_gate_e[l], w_up_e[l], w_down_e[l], n_y_tiles)
    out = _final_ln(h1t, y, gw, row(ln2_g[l]), row(ln2_b[l]))
    return out.reshape(nbatch, seq, d)
```

```python
import functools
import math

import jax
import jax.numpy as jnp
from jax import lax
from jax.experimental import pallas as pl
from jax.experimental.pallas import tpu as pltpu

F32 = jnp.float32
BF16 = jnp.bfloat16
I32 = jnp.int32
U32 = jnp.uint32

D_MODEL = 1024
SEQ = 2048
HEAD_DIM = 64
N_Q_HEADS = 16
N_KV_HEADS = 2
ATTN_BLOCK = 128
WINDOW = 128
NUM_BUCKETS = 32
MAX_DISTANCE = 128
D_INNER = 2048
N_SSM_HEADS = 32
N_SSM_GROUPS = 4
D_STATE = 128
CONV_WIDTH = 4
CHUNK = 128
N_EXPERT_GROUPS = 4
EXPERTS_PER_GROUP = 8
N_EXPERTS = 32
D_EXPERT = 512
DEEPNORM_ALPHA = 2.0 ** 0.25
LN_EPS = 1e-5

LANES = 128
SUBLANES = 8

COL_Z = 0
COL_XS = 2048
COL_B = 4096
COL_C = 4608
COL_Q = 5120
COL_GA = 6144
COL_GS = 7168
COL_KV = 8192
N_PROJ = 8704

TM_PROJ = 512
CH_PROJ = 256
CONV_CH = D_INNER + 2 * N_SSM_GROUPS * D_STATE
RAW_SLOTS = 3
Q_BLOCKS_ATTN = 4
CHUNKS_SSD = 4
TM_MIX = 256
TM_FIN = 512
BM_MOE = 512
HALF_MOE = 256
PARTS_MOE = BM_MOE // HALF_MOE
TD_DISPATCH = 512
DMA_PRIORITIES = 2

VMEM_LIMIT = 56 * 1024 * 1024


def _layer_norm(x, g, b):
    mu = jnp.mean(x, axis=-1, keepdims=True)
    xc = x - mu
    var = jnp.mean(xc * xc, axis=-1, keepdims=True)
    return xc * lax.rsqrt(var + LN_EPS) * g + b


assert D_MODEL == SUBLANES * LANES


def _store_token_tiles(ref, val, n):
    for c in range(SUBLANES):
        ref[pl.ds(c, n, stride=SUBLANES), :] = val[:, c * LANES:(c + 1) * LANES]


def _load_token_tiles(ref, n):
    return jnp.concatenate([ref[pl.ds(c, n, stride=SUBLANES), :] for c in range(SUBLANES)], axis=1)


def _softplus(x):
    return jnp.maximum(x, 0.0) + jnp.log1p(jnp.exp(-jnp.abs(x)))


def _silu(x):
    return x * jax.nn.sigmoid(x)


def _ln_inproj_kernel(x_ref, g_ref, b_ref, w_hbm, wdt_ref, cw_ref, cb_ref, bg_ref, dtb_ref,
                      o_ref, dt_ref, w_scr, carry_scr, raw_scr, sem):
    i = pl.program_id(0)

    @pl.when(i == 0)
    def _():
        cp = pltpu.make_async_copy(w_hbm, w_scr, sem)
        cp.start()
        carry_scr[...] = jnp.zeros_like(carry_scr)
        cp.wait()

    hb = _layer_norm(x_ref[...], g_ref[...], b_ref[...]).astype(BF16)
    tm = hb.shape[0]

    seq_start = lax.rem(i, SEQ // TM_PROJ) == 0
    sub = lax.broadcasted_iota(I32, (tm // SUBLANES, SUBLANES, CH_PROJ), 1)

    def conv_silu(r, c0):
        cc = slice(c0 - COL_XS, c0 - COL_XS + CH_PROJ)
        prev = jnp.where(seq_start, 0.0, carry_scr[:, cc])
        carry_scr[:, cc] = r[tm - SUBLANES:]
        acc = cb_ref[:, cc] + cw_ref[CONV_WIDTH - 1:CONV_WIDTH, cc] * r
        r3 = jnp.concatenate([prev, r], axis=0).reshape(tm // SUBLANES + 1, SUBLANES, CH_PROJ)
        for k in range(1, CONV_WIDTH):
            rot = pltpu.roll(r3, k, axis=1)
            rk = jnp.where(sub < k, rot[:-1], rot[1:]).reshape(tm, CH_PROJ)
            acc = acc + cw_ref[CONV_WIDTH - 1 - k:CONV_WIDTH - k, cc] * rk
        return _silu(acc)

    def epilogue(r, c0):
        if c0 < COL_XS:
            return _silu(r)
        if c0 < COL_Q:
            return conv_silu(r, c0)
        if COL_GA <= c0 < COL_KV:
            return jax.nn.sigmoid(r + bg_ref[:, c0 - COL_GA:c0 - COL_GA + CH_PROJ])
        return r

    def proj(c0):
        return jnp.dot(hb, w_scr[:, c0:c0 + CH_PROJ], preferred_element_type=F32)

    heavy = list(range(COL_XS, COL_Q, CH_PROJ))
    light = [c for c in range(0, N_PROJ, CH_PROJ) if c not in heavy]
    starts = []
    while heavy or light:
        starts += heavy[:1] + light[:2]
        heavy, light = heavy[1:], light[2:]
    zero = jnp.minimum(i, 0)
    ahead = RAW_SLOTS - 1
    for n in range(ahead):
        raw_scr[zero + n] = proj(starts[n])
    for n, c0 in enumerate(starts):
        if n + ahead < len(starts):
            raw_scr[zero + (n + ahead) % RAW_SLOTS] = proj(starts[n + ahead])
        o_ref[:, c0:c0 + CH_PROJ] = epilogue(raw_scr[zero + n % RAW_SLOTS], c0).astype(BF16)
    dt_ref[...] = _softplus(jnp.dot(hb, wdt_ref[...], preferred_element_type=F32) + dtb_ref[...])


def _ln_inproj(x2, g, b, w_cat, w_dt, conv_w, conv_b, b_gate, dt_bias):
    t = x2.shape[0]
    const = lambda i: (0, 0)
    return pl.pallas_call(
        _ln_inproj_kernel,
        grid=(t // TM_PROJ,),
        in_specs=[
            pl.BlockSpec((TM_PROJ, D_MODEL), lambda i: (i, 0)),
            pl.BlockSpec((1, D_MODEL), const),
            pl.BlockSpec((1, D_MODEL), const),
            pl.BlockSpec(memory_space=pl.ANY),
            pl.BlockSpec((D_MODEL, LANES), const),
            pl.BlockSpec((CONV_WIDTH, CONV_CH), const),
            pl.BlockSpec((1, CONV_CH), const),
            pl.BlockSpec((1, 2 * D_MODEL), const),
            pl.BlockSpec((1, LANES), const),
        ],
        out_specs=[
            pl.BlockSpec((TM_PROJ, N_PROJ), lambda i: (i, 0)),
            pl.BlockSpec((TM_PROJ, LANES), lambda i: (i, 0)),
        ],
        out_shape=[
            jax.ShapeDtypeStruct((t, N_PROJ), BF16),
            jax.ShapeDtypeStruct((t, LANES), F32),
        ],
        scratch_shapes=[
            pltpu.VMEM((D_MODEL, N_PROJ), BF16),
            pltpu.VMEM((SUBLANES, CONV_CH), F32),
            pltpu.VMEM((RAW_SLOTS, TM_PROJ, CH_PROJ), F32),
            pltpu.SemaphoreType.DMA(()),
        ],
        compiler_params=pltpu.CompilerParams(
            dimension_semantics=("arbitrary",), vmem_limit_bytes=VMEM_LIMIT),
        name="ln_inproj",
    )(x2, g, b, w_cat, w_dt, conv_w, conv_b, b_gate, dt_bias)


def _attn_kernel(sink_ref, q_ref, kvc_ref, kvp_ref, bias_ref, o_ref):
    first = jnp.where(pl.program_id(1) == 0, 1, 0)
    lo = lax.broadcasted_iota(I32, (ATTN_BLOCK, LANES), 1) < HEAD_DIM
    for qb in range(Q_BLOCKS_ATTN):
        rows = slice(qb * ATTN_BLOCK, (qb + 1) * ATTN_BLOCK)
        for kv in range(N_KV_HEADS):
            ks = slice(kv * LANES, (kv + 1) * LANES)
            vs = slice((N_KV_HEADS + kv) * LANES, (N_KV_HEADS + kv + 1) * LANES)
            if qb == 0:
                k_prev, v_prev, table = kvp_ref[:, ks], kvp_ref[:, vs], first
            else:
                before = slice((qb - 1) * ATTN_BLOCK, qb * ATTN_BLOCK)
                k_prev, v_prev, table = kvc_ref[before, ks], kvc_ref[before, vs], 0
            kcat = jnp.concatenate([k_prev, kvc_ref[rows, ks]], axis=0)
            vcat = jnp.concatenate([v_prev, kvc_ref[rows, vs]], axis=0)
            for pp in range(4):
                p = kv * 4 + pp
                qp = q_ref[rows, p * LANES:(p + 1) * LANES].astype(F32)
                qm = jnp.concatenate([jnp.where(lo, qp, 0.0), jnp.where(lo, 0.0, qp)], axis=0).astype(BF16)
                s = lax.dot_general(qm, kcat, (((1,), (1,)), ((), ())), preferred_element_type=F32)
                outs = []
                for hh in range(2):
                    h = 2 * p + hh
                    sh = s[hh * ATTN_BLOCK:(hh + 1) * ATTN_BLOCK] + bias_ref[table, h]
                    snk = sink_ref[h]
                    m = jnp.maximum(jnp.max(sh, axis=-1, keepdims=True), snk)
                    pe = jnp.exp(sh - m)
                    den = jnp.sum(pe, axis=-1, keepdims=True) + jnp.exp(snk - m)
                    o = jnp.dot(pe.astype(BF16), vcat, preferred_element_type=F32)
                    outs.append(o / den)
                o_ref[rows, p * LANES:(p + 1) * LANES] = jnp.where(lo, outs[0], outs[1]).astype(BF16)


def _swa_attn(proj, bias2, sink, nbatch):
    t = proj.shape[0]
    rows_step = Q_BLOCKS_ATTN * ATTN_BLOCK
    nb = SEQ // rows_step
    q_blk = COL_Q // D_MODEL
    kv_blk = COL_KV // 512
    return pl.pallas_call(
        _attn_kernel,
        grid=(nbatch, nb),
        in_specs=[
            pl.BlockSpec(memory_space=pltpu.SMEM),
            pl.BlockSpec((rows_step, D_MODEL), lambda b, i: (b * nb + i, q_blk)),
            pl.BlockSpec((rows_step, 512), lambda b, i: (b * nb + i, kv_blk)),
            pl.BlockSpec((ATTN_BLOCK, 512),
                         lambda b, i: ((b * nb + i) * Q_BLOCKS_ATTN - jnp.minimum(i, 1), kv_blk)),
            pl.BlockSpec((2, N_Q_HEADS, ATTN_BLOCK, 2 * ATTN_BLOCK), lambda b, i: (0, 0, 0, 0)),
        ],
        out_specs=pl.BlockSpec((rows_step, D_MODEL), lambda b, i: (b * nb + i, 0)),
        out_shape=jax.ShapeDtypeStruct((t, D_MODEL), BF16),
        compiler_params=pltpu.CompilerParams(
            dimension_semantics=("arbitrary", "arbitrary"), vmem_limit_bytes=VMEM_LIMIT),
        name="swa_attn",
    )(sink, proj, proj, proj, bias2)


def _ssd_kernel(zs_ref, xs_ref, bc_ref, dt_ref, a_ref, dsk_ref, ng_ref, y_ref, state_scr):
    @pl.when(pl.program_id(1) == 0)
    def _():
        state_scr[...] = jnp.zeros_like(state_scr)

    chunks = [pl.ds(ck * CHUNK, CHUNK) for ck in range(CHUNKS_SSD)]
    terms = [_ssd_decay_terms(dt_ref.at[rows, :], a_ref) for rows in chunks]
    for rows, tm in zip(chunks, terms):
        _ssd_chunk(zs_ref.at[rows, :], xs_ref.at[rows, :], bc_ref.at[rows, :], tm,
                   dsk_ref, ng_ref, y_ref.at[rows, :], state_scr)


def _ssd_decay_terms(dt_ref, a_ref):
    dt = dt_ref[...]
    da = dt * a_ref[...]
    row = lax.broadcasted_iota(I32, (CHUNK, CHUNK), 0)
    col = lax.broadcasted_iota(I32, (CHUNK, CHUNK), 1)
    tril = jnp.where(row >= col, 1.0, 0.0).astype(BF16)
    hi = da.astype(BF16)
    r1 = da - hi.astype(F32)
    mid = r1.astype(BF16)
    low = (r1 - mid.astype(F32)).astype(BF16)
    acum = (jnp.dot(tril, hi, preferred_element_type=F32) + jnp.dot(tril, mid, preferred_element_type=F32)
            + jnp.dot(tril, low, preferred_element_type=F32))
    dtw = dt * jnp.exp(acum[CHUNK - 1:CHUNK, :] - acum)
    return acum, acum.T, dt.T, dtw


def _ssd_chunk(zs_ref, xs_ref, bc_ref, terms, dsk_ref, ng_ref, y_ref, state_scr):
    bm = bc_ref[:, :N_SSM_GROUPS * D_STATE]
    cm = bc_ref[:, N_SSM_GROUPS * D_STATE:]
    acum, acum_t, dt_t, dtw = terms
    causal = lax.broadcasted_iota(I32, (CHUNK, CHUNK), 0) >= lax.broadcasted_iota(I32, (CHUNK, CHUNK), 1)

    def over_lanes(v, e):
        return jnp.broadcast_to(v[:, e:e + 1], (CHUNK, LANES))

    def over_rows(v, e):
        return jnp.broadcast_to(v[e:e + 1, :], (CHUNK, LANES))

    lo = lax.broadcasted_iota(I32, (CHUNK, LANES), 1) < (LANES // 2)
    gw = D_INNER // N_SSM_GROUPS
    pairs = gw // LANES
    for g in range(N_SSM_GROUPS):
        c_g = cm[:, g * D_STATE:(g + 1) * D_STATE]
        b_g = bm[:, g * D_STATE:(g + 1) * D_STATE]
        cb = lax.dot_general(c_g, b_g, (((1,), (1,)), ((), ())), preferred_element_type=F32)
        cb = jnp.where(causal, cb, 0.0)
        st_g = state_scr[:, g * gw:(g + 1) * gw]
        y_off = jnp.dot(c_g, st_g.astype(BF16), preferred_element_type=F32)
        hg, xw, ea_end = [], [], []
        for jj in range(pairs):
            j = g * pairs + jj
            cs = slice(j * LANES, (j + 1) * LANES)
            xp_bf = xs_ref[:, cs]
            xp = xp_bf.astype(F32)
            ms, ea, wl = [], [], []
            for e in (2 * j, 2 * j + 1):
                a_col = over_lanes(acum, e)
                dec = jnp.exp(jnp.minimum(a_col - over_rows(acum_t, e), 0.0))
                ms.append((cb * dec * over_rows(dt_t, e)).astype(BF16))
                ea.append(jnp.exp(a_col))
                wl.append(over_lanes(dtw, e))
            zero = jnp.zeros_like(xp_bf)
            rhs = jnp.concatenate([jnp.where(lo, xp_bf, zero), jnp.where(lo, zero, xp_bf)], axis=0)
            y_diag = jnp.dot(jnp.concatenate(ms, axis=1), rhs, preferred_element_type=F32)
            ea_p = jnp.where(lo, ea[0], ea[1])
            y_p = y_diag + y_off[:, jj * LANES:(jj + 1) * LANES] * ea_p + dsk_ref[:, cs] * xp
            hg.append(y_p * zs_ref[:, cs].astype(F32))
            xw.append((xp * jnp.where(lo, wl[0], wl[1])).astype(BF16))
            ea_end.append(ea_p[CHUNK - 1:CHUNK, :])
        b_t = b_g.astype(F32).T.astype(BF16)
        upd = jnp.dot(b_t, jnp.concatenate(xw, axis=1), preferred_element_type=F32)
        state_scr[:, g * gw:(g + 1) * gw] = st_g * jnp.concatenate(ea_end, axis=1) + upd
        h_g = jnp.concatenate(hg, axis=1)
        ms_g = jnp.mean(h_g * h_g, axis=-1, keepdims=True)
        y_ref[:, g * gw:(g + 1) * gw] = (h_g * lax.rsqrt(ms_g + LN_EPS) * ng_ref[:, g * gw:(g + 1) * gw]).astype(BF16)


def _ssd(proj, dt, a_neg, dskip_b, norm_g, nbatch):
    t = proj.shape[0]
    rows_step = CHUNKS_SSD * CHUNK
    nc = SEQ // rows_step
    const = lambda b, c: (0, 0)
    return pl.pallas_call(
        _ssd_kernel,
        grid=(nbatch, nc),
        in_specs=[
            pl.BlockSpec((rows_step, D_INNER), lambda b, c: (b * nc + c, COL_Z // D_INNER)),
            pl.BlockSpec((rows_step, D_INNER), lambda b, c: (b * nc + c, COL_XS // D_INNER)),
            pl.BlockSpec((rows_step, 1024), lambda b, c: (b * nc + c, COL_B // 1024)),
            pl.BlockSpec((rows_step, LANES), lambda b, c: (b * nc + c, 0)),
            pl.BlockSpec((1, LANES), const),
            pl.BlockSpec((1, D_INNER), const),
            pl.BlockSpec((1, D_INNER), const),
        ],
        out_specs=pl.BlockSpec((rows_step, D_INNER), lambda b, c: (b * nc + c, 0)),
        out_shape=jax.ShapeDtypeStruct((t, D_INNER), BF16),
        scratch_shapes=[pltpu.VMEM((D_STATE, D_INNER), F32)],
        compiler_params=pltpu.CompilerParams(
            dimension_semantics=("arbitrary", "arbitrary"), vmem_limit_bytes=VMEM_LIMIT),
        name="ssd",
    )(proj, proj, proj, dt, a_neg, dskip_b, norm_g)


def _mix_route_kernel(attn_ref, y_ref, gate_ref, x_ref, lng_ref, lnb_ref, wao_ref, wso_ref, wo_ref,
                      l1g_ref, l1b_ref, wr_ref, h1t_ref, eid_ref, gw_ref, cnt_ref, mix_scr, stage_scr, logit_scr):
    i = pl.program_id(0)

    @pl.when(i == 0)
    def _():
        cnt_ref[...] = jnp.zeros_like(cnt_ref)
        mix_scr[...] = jnp.zeros_like(mix_scr)
        logit_scr[...] = jnp.zeros_like(logit_scr)

    logits = logit_scr[...]
    lane = lax.broadcasted_iota(I32, (TM_MIX, LANES), 1)
    neg = -jnp.inf
    lg = jnp.where(lane < N_EXPERT_GROUPS, logits, neg)
    mg = jnp.max(lg, axis=-1, keepdims=True)
    grp = jnp.min(jnp.where(lg == mg, lane, LANES), axis=-1, keepdims=True)
    p_grp = 1.0 / jnp.sum(jnp.exp(lg - mg), axis=-1, keepdims=True)
    el = lane - N_EXPERT_GROUPS
    in_grp = (el >= 0) & (el < N_EXPERTS) & (lax.shift_right_arithmetic(el, 3) == grp)
    le = jnp.where(in_grp, logits, neg)
    m1 = jnp.max(le, axis=-1, keepdims=True)
    i1 = jnp.min(jnp.where(le == m1, lane, LANES), axis=-1, keepdims=True)
    le2 = jnp.where(lane == i1, neg, le)
    m2 = jnp.max(le2, axis=-1, keepdims=True)
    i2 = jnp.min(jnp.where(le2 == m2, lane, LANES), axis=-1, keepdims=True)
    e2 = jnp.exp(m2 - m1)
    p1 = p_grp / (1.0 + e2)
    p2 = p1 * e2
    gw_ref[...] = jnp.where(lane == 0, p1, jnp.where(lane == 1, p2, 0.0))

    pick = (lane == i1) | (lane == i2)
    row = lax.broadcasted_iota(I32, (TM_MIX, LANES), 0)
    ones = jnp.where(pick, 1.0, 0.0)
    seen = ones
    shift = 1
    while shift < TM_MIX:
        seen = seen + jnp.where(row >= shift, pltpu.roll(seen, shift, axis=0), 0.0)
        shift *= 2
    before = seen - ones + cnt_ref[...].astype(F32)
    rank1 = jnp.sum(jnp.where(lane == i1, before, 0.0), axis=-1, keepdims=True).astype(I32)
    rank2 = jnp.sum(jnp.where(lane == i2, before, 0.0), axis=-1, keepdims=True).astype(I32)
    eid_ref[...] = jnp.where(lane == 0, i1 - N_EXPERT_GROUPS,
                             jnp.where(lane == 1, i2 - N_EXPERT_GROUPS,
                                       jnp.where(lane == 2, rank1, jnp.where(lane == 3, rank2, 0))))
    picked = jnp.where(pick & (i > 1), 1, 0)
    cnt_ref[...] += jnp.sum(picked, axis=0, keepdims=True)

    slot = lax.rem(i, 2)
    mix_prev = mix_scr[1 - slot]
    zero = jnp.minimum(i, 0)
    stage_scr[zero] = jnp.dot(attn_ref[...], wao_ref[...], preferred_element_type=F32)
    stage_scr[zero + 1] = jnp.dot(y_ref[...], wso_ref[...], preferred_element_type=F32)
    gate = gate_ref[...].astype(F32)
    mixin = (gate[:, :D_MODEL] * stage_scr[zero] + gate[:, D_MODEL:] * stage_scr[zero + 1]).astype(BF16)
    mix_scr[slot] = jnp.dot(mixin, wo_ref[...], preferred_element_type=F32)

    h = _layer_norm(x_ref[...], lng_ref[...], lnb_ref[...])
    h1 = _layer_norm(DEEPNORM_ALPHA * h + mix_prev, l1g_ref[...], l1b_ref[...])
    _store_token_tiles(h1t_ref, h1, TM_MIX)
    logit_scr[...] = jnp.dot(h1.astype(BF16), wr_ref[...], preferred_element_type=F32)


def _mix_route(attn, yssm, proj, x2, lng, lnb, w_ao, w_so, w_o, l1g, l1b, w_r):
    t = x2.shape[0]
    nt = t // TM_MIX
    const = lambda i: (0, 0)
    row = lambda i: (jnp.minimum(i, nt - 1), 0)
    prev = lambda i: (jnp.clip(i - 1, 0, nt - 1), 0)
    prev2 = lambda i: (jnp.maximum(i - 2, 0), 0)
    return pl.pallas_call(
        _mix_route_kernel,
        grid=(nt + 2,),
        in_specs=[
            pl.BlockSpec((TM_MIX, D_MODEL), row),
            pl.BlockSpec((TM_MIX, D_INNER), row),
            pl.BlockSpec((TM_MIX, 2 * D_MODEL), lambda i: (jnp.minimum(i, nt - 1), COL_GA // (2 * D_MODEL))),
            pl.BlockSpec((TM_MIX, D_MODEL), prev),
            pl.BlockSpec((1, D_MODEL), const),
            pl.BlockSpec((1, D_MODEL), const),
            pl.BlockSpec((D_MODEL, D_MODEL), const),
            pl.BlockSpec((D_INNER, D_MODEL), const),
            pl.BlockSpec((D_MODEL, D_MODEL), const),
            pl.BlockSpec((1, D_MODEL), const),
            pl.BlockSpec((1, D_MODEL), const),
            pl.BlockSpec((D_MODEL, LANES), const),
        ],
        out_specs=[
            pl.BlockSpec((TM_MIX * SUBLANES, LANES), prev),
            pl.BlockSpec((TM_MIX, LANES), prev2),
            pl.BlockSpec((TM_MIX, LANES), prev2),
            pl.BlockSpec((1, LANES), const),
        ],
        out_shape=[
            jax.ShapeDtypeStruct((t * SUBLANES, LANES), F32),
            jax.ShapeDtypeStruct((t, LANES), I32),
            jax.ShapeDtypeStruct((t, LANES), F32),
            jax.ShapeDtypeStruct((1, LANES), I32),
        ],
        scratch_shapes=[pltpu.VMEM((2, TM_MIX, D_MODEL), F32), pltpu.VMEM((2, TM_MIX, D_MODEL), F32),
                        pltpu.VMEM((TM_MIX, LANES), F32)],
        compiler_params=pltpu.CompilerParams(
            dimension_semantics=("arbitrary",), vmem_limit_bytes=VMEM_LIMIT),
        name="mix_route",
    )(attn, yssm, proj, x2, lng, lnb, w_ao, w_so, w_o, l1g, l1b, w_r)


def _dispatch_kernel(dest_ref, h_ref, z_ref, xr_hbm, sem, *, n_token_steps):
    def copy_all(src_ref):
        for j in range(TD_DISPATCH):
            for k in range(2):
                r8 = pl.multiple_of(dest_ref[0, 0, 2 * j + k], SUBLANES)
                pltpu.make_async_copy(src_ref.at[pl.ds(j * SUBLANES, SUBLANES), :],
                                      xr_hbm.at[pl.ds(r8, SUBLANES), :], sem).start(priority=k)

    @pl.when(pl.program_id(0) < n_token_steps)
    def _():
        copy_all(h_ref)

    @pl.when(pl.program_id(0) >= n_token_steps)
    def _():
        copy_all(z_ref)

    for k in range(2):
        pltpu.make_async_copy(h_ref, xr_hbm.at[pl.ds(0, TD_DISPATCH * SUBLANES), :], sem).wait()


def _dispatch(dest8, h1t):
    n_rows = dest8.shape[0]
    nt = h1t.shape[0] // (TD_DISPATCH * SUBLANES)
    n_steps = n_rows // (2 * TD_DISPATCH)
    zeros = jnp.zeros((TD_DISPATCH * SUBLANES, LANES), F32)
    return pl.pallas_call(
        functools.partial(_dispatch_kernel, n_token_steps=nt),
        grid=(n_steps,),
        in_specs=[
            pl.BlockSpec((1, 1, 2 * TD_DISPATCH), lambda i: (i, 0, 0), memory_space=pltpu.SMEM),
            pl.BlockSpec((TD_DISPATCH * SUBLANES, LANES), lambda i: (jnp.minimum(i, nt - 1), 0)),
            pl.BlockSpec((TD_DISPATCH * SUBLANES, LANES), lambda i: (0, 0)),
        ],
        out_specs=pl.BlockSpec(memory_space=pl.ANY),
        out_shape=jax.ShapeDtypeStruct((n_rows * SUBLANES, LANES), F32),
        scratch_shapes=[pltpu.SemaphoreType.DMA(())],
        compiler_params=pltpu.CompilerParams(
            dimension_semantics=("arbitrary",), vmem_limit_bytes=VMEM_LIMIT),
        name="dispatch",
    )(dest8.reshape(n_steps, 1, 2 * TD_DISPATCH), h1t, zeros)


def _moe_kernel(be_ref, dst_ref, x_ref, wg_ref, wu_ref, wd_ref, y_hbm, ybuf, wg_bf, wu_bf, wd_bf, ssem):
    i = pl.program_id(0)
    nblk = pl.num_programs(0)
    part_rows = HALF_MOE * SUBLANES

    def issue_scatter(s):
        for j in range(HALF_MOE):
            r8 = pl.multiple_of(dst_ref[0, 0, s * HALF_MOE + j], SUBLANES)
            pltpu.make_async_copy(ybuf.at[s, pl.ds(j * SUBLANES, SUBLANES), :], y_hbm.at[pl.ds(r8, SUBLANES), :],
                                  ssem.at[s]).start(priority=j % DMA_PRIORITIES)

    def wait_scatter(s):
        pltpu.make_async_copy(ybuf.at[s], y_hbm.at[pl.ds(0, part_rows), :], ssem.at[s]).wait()

    @pl.when((i == 0) | (be_ref[i] != be_ref[jnp.maximum(i - 1, 0)]))
    def _():
        wg_bf[...] = wg_ref[0].astype(BF16)
        wu_bf[...] = wu_ref[0].astype(BF16)
        wd_bf[...] = wd_ref[0].astype(BF16)

    for s in range(PARTS_MOE):
        @pl.when(i > 0)
        def _():
            wait_scatter(s)

        x = _load_token_tiles(x_ref.at[pl.ds(s * part_rows, part_rows), :], HALF_MOE).astype(BF16)
        gt = jnp.dot(x, wg_bf[...], preferred_element_type=F32)
        up = jnp.dot(x, wu_bf[...], preferred_element_type=F32)
        hid = (gt * jax.nn.sigmoid(gt) * up).astype(BF16)
        y = jnp.dot(hid, wd_bf[...], preferred_element_type=F32)
        _store_token_tiles(ybuf.at[s], y, HALF_MOE)
        issue_scatter(s)

    @pl.when(i == nblk - 1)
    def _():
        for s in range(PARTS_MOE):
            wait_scatter(s)


def _moe(block_e, dst8, xr, w_gate, w_up, w_down, n_y_tiles):
    nblk = dst8.shape[0] // BM_MOE
    grid_spec = pltpu.PrefetchScalarGridSpec(
        num_scalar_prefetch=1,
        grid=(nblk,),
        in_specs=[
            pl.BlockSpec((1, 1, BM_MOE), lambda i, be: (i, 0, 0), memory_space=pltpu.SMEM),
            pl.BlockSpec((BM_MOE * SUBLANES, LANES), lambda i, be: (i, 0)),
            pl.BlockSpec((1, D_MODEL, D_EXPERT), lambda i, be: (be[i], 0, 0)),
            pl.BlockSpec((1, D_MODEL, D_EXPERT), lambda i, be: (be[i], 0, 0)),
            pl.BlockSpec((1, D_EXPERT, D_MODEL), lambda i, be: (be[i], 0, 0)),
        ],
        out_specs=pl.BlockSpec(memory_space=pl.ANY),
        scratch_shapes=[
            pltpu.VMEM((PARTS_MOE, HALF_MOE * SUBLANES, LANES), F32),
            pltpu.VMEM((D_MODEL, D_EXPERT), BF16),
            pltpu.VMEM((D_MODEL, D_EXPERT), BF16),
            pltpu.VMEM((D_EXPERT, D_MODEL), BF16),
            pltpu.SemaphoreType.DMA((PARTS_MOE,)),
        ],
    )
    return pl.pallas_call(
        _moe_kernel,
        grid_spec=grid_spec,
        out_shape=jax.ShapeDtypeStruct((n_y_tiles * SUBLANES, LANES), F32),
        compiler_params=pltpu.CompilerParams(
            dimension_semantics=("arbitrary",), vmem_limit_bytes=VMEM_LIMIT),
        name="moe",
    )(block_e, dst8.reshape(nblk, 1, BM_MOE), xr, w_gate, w_up, w_down)


def _final_kernel(h1t_ref, y0_ref, y1_ref, gw_ref, g_ref, b_ref, o_ref):
    n = o_ref.shape[0]
    gw = gw_ref[...]
    ffn = gw[:, 0:1] * _load_token_tiles(y0_ref, n) + gw[:, 1:2] * _load_token_tiles(y1_ref, n)
    o_ref[...] = _layer_norm(DEEPNORM_ALPHA * _load_token_tiles(h1t_ref, n) + ffn, g_ref[...], b_ref[...])


def _final_ln(h1t, y, gw, g, b):
    t = gw.shape[0]
    nt = t // TM_FIN
    const = lambda i: (0, 0)
    tiles = (TM_FIN * SUBLANES, LANES)
    return pl.pallas_call(
        _final_kernel,
        grid=(nt,),
        in_specs=[
            pl.BlockSpec(tiles, lambda i: (i, 0)),
            pl.BlockSpec(tiles, lambda i: (i, 0)),
            pl.BlockSpec(tiles, lambda i: (i + nt, 0)),
            pl.BlockSpec((TM_FIN, LANES), lambda i: (i, 0)),
            pl.BlockSpec((1, D_MODEL), const),
            pl.BlockSpec((1, D_MODEL), const),
        ],
        out_specs=pl.BlockSpec((TM_FIN, D_MODEL), lambda i: (i, 0)),
        out_shape=jax.ShapeDtypeStruct((t, D_MODEL), F32),
        compiler_params=pltpu.CompilerParams(
            dimension_semantics=("arbitrary",), vmem_limit_bytes=VMEM_LIMIT),
        name="final_ln",
    )(h1t, y, y, gw, g, b)


def _t5_causal_bucket(dist):
    max_exact = NUM_BUCKETS // 2
    d_f = jnp.maximum(dist, 1).astype(F32)
    large = max_exact + (jnp.log(d_f / max_exact) / math.log(MAX_DISTANCE / max_exact)
                         * (NUM_BUCKETS - max_exact)).astype(I32)
    large = jnp.minimum(large, NUM_BUCKETS - 1)
    return jnp.where(dist < max_exact, dist, large)


def _attn_bias_tables(rel_bias):
    qi = jnp.arange(ATTN_BLOCK)[:, None]
    kj = jnp.arange(2 * ATTN_BLOCK)[None, :]
    dist = qi + ATTN_BLOCK - kj
    bucket = _t5_causal_bucket(jnp.clip(dist, 0, None))
    hit = bucket[None, None] == jnp.arange(NUM_BUCKETS, dtype=I32)[None, :, None, None]
    bias = jnp.sum(jnp.where(hit, rel_bias.astype(F32).T[:, :, None, None], 0.0), axis=1)
    in_window = (dist >= 0) & (dist < WINDOW)
    normal = jnp.where(in_window[None], bias, -jnp.inf)
    first = jnp.where((in_window & (kj >= ATTN_BLOCK))[None], bias, -jnp.inf)
    return jnp.stack([normal, first], axis=0)


ASSIGN_BITS = 17


def _route_metadata(eid, rank, counts, t):
    n_assign = 2 * t
    assert n_assign <= 1 << ASSIGN_BITS
    n_blk = n_assign // BM_MOE + N_EXPERTS
    padded = (counts + BM_MOE - 1) // BM_MOE * BM_MOE
    pends = jnp.cumsum(padded)
    n_pad = padded - counts
    experts = jnp.arange(N_EXPERTS, dtype=I32)
    start_of = jnp.sum(jnp.where(eid[..., None] == experts, pends - padded, 0), axis=-1)
    j = jnp.arange(BM_MOE, dtype=I32)[None, :]
    ee = jnp.arange(N_EXPERTS, dtype=I32)[:, None]
    is_tail = (j < n_pad[:, None]).reshape(-1)
    tail_row = ((pends - n_pad)[:, None] + j).reshape(-1)
    spare_row = pends[-1] + jnp.cumsum(jnp.where(is_tail, 0, 1)) - 1
    pad_rows = jnp.where(is_tail, tail_row, spare_row)
    dest8 = (jnp.concatenate([(start_of + rank).reshape(-1), pad_rows]) * SUBLANES).astype(I32)
    key = (eid.reshape(-1) << (ASSIGN_BITS + 1)) | jnp.arange(n_assign, dtype=I32)
    pad_key = jnp.where(j < n_pad[:, None], (ee << (ASSIGN_BITS + 1)) | (1 << ASSIGN_BITS) | j,
                        jnp.iinfo(jnp.int32).max)
    keys = jnp.sort(jnp.concatenate([key, pad_key.reshape(-1)]))
    real = (keys & (1 << ASSIGN_BITS)) == 0
    a = keys & ((1 << ASSIGN_BITS) - 1)
    tok = a >> 1
    r = jnp.arange(n_blk * BM_MOE, dtype=I32)
    dst8 = jnp.where(real, (a & 1) * t + tok, n_assign + (r % BM_MOE)) * SUBLANES
    blk_start = jnp.arange(n_blk, dtype=I32)[:, None] * BM_MOE
    block_e = jnp.minimum(jnp.sum((pends[None, :] <= blk_start).astype(I32), axis=1), N_EXPERTS - 1)
    return block_e.astype(I32), dest8, dst8.astype(I32), n_assign + BM_MOE


def kernel(x, ln_in_g, ln_in_b, rel_bias, w_in, b_gate, attn_sink, conv_w, conv_b, dt_bias, a_log, d_skip,
           ssm_norm_g, w_attn_out, w_ssm_out, w_out, ln1_g, ln1_b, w_group_router, w_expert_router,
           w_gate_e, w_up_e, w_down_e, ln2_g, ln2_b):
    nbatch, seq, d = x.shape
    assert seq == SEQ and d == D_MODEL
    t = nbatch * seq
    x2 = x.reshape(t, d)
    l = 0

    w = w_in[l]
    q_w, k_w, v_w, z_w, xs_w, b_w, c_w, dt_w, ga_w, gs_w = jnp.split(
        w, [1024, 1152, 1280, 3328, 5376, 5888, 6400, 6432, 7456], axis=1)
    k0, k1 = k_w[:, :HEAD_DIM], k_w[:, HEAD_DIM:]
    v0, v1 = v_w[:, :HEAD_DIM], v_w[:, HEAD_DIM:]
    w_cat = jnp.concatenate([z_w, xs_w, b_w, c_w, q_w * (HEAD_DIM ** -0.5), ga_w, gs_w,
                             k0, k0, k1, k1, v0, v0, v1, v1], axis=1).astype(BF16)
    w_dt = jnp.pad(dt_w, ((0, 0), (0, LANES - N_SSM_HEADS))).astype(BF16)
    row = lambda v: v.reshape(1, -1).astype(F32)
    pad_heads = lambda v: jnp.pad(v.astype(F32), (0, LANES - N_SSM_HEADS)).reshape(1, LANES)
    a_neg = pad_heads(-jnp.exp(a_log[l].astype(F32)))
    dskip_b = jnp.repeat(d_skip[l].astype(F32), D_INNER // N_SSM_HEADS).reshape(1, D_INNER)
    w_r = jnp.pad(jnp.concatenate([w_group_router[l], w_expert_router[l]], axis=1),
                  ((0, 0), (0, LANES - N_EXPERT_GROUPS - N_EXPERTS))).astype(BF16)
    bias2 = _attn_bias_tables(rel_bias)

    proj, dt = _ln_inproj(x2, row(ln_in_g), row(ln_in_b), w_cat, w_dt, conv_w[l].astype(F32), row(conv_b[l]),
                          row(b_gate[l]), pad_heads(dt_bias[l]))
    attn = _swa_attn(proj, bias2, attn_sink[l].astype(F32), nbatch)
    yssm = _ssd(proj, dt, a_neg, dskip_b, row(ssm_norm_g[l]), nbatch)
    h1t, eid, gw, cnt = _mix_route(attn, yssm, proj, x2, row(ln_in_g), row(ln_in_b),
                                   w_attn_out[l].astype(BF16), w_ssm_out[l].astype(BF16), w_out[l].astype(BF16),
                                   row(ln1_g[l]), row(ln1_b[l]), w_r)

    counts = cnt[0, N_EXPERT_GROUPS:N_EXPERT_GROUPS + N_EXPERTS]
    block_e, dest8, dst8, n_y_tiles = _route_metadata(eid[:, :2], eid[:, 2:4], counts, t)
    xr = _dispatch(dest8, h1t)
    y = _moe(block_e, dst8, xr, w_gate_e[l], w_up_e[l], w_down_e[l], n_y_tiles)
    out = _final_ln(h1t, y, gw, row(ln2_g[l]), row(ln2_b[l]))
    return out.reshape(nbatch, seq, d)
```

```python
import functools
import math

import jax
import jax.numpy as jnp
from jax import lax
from jax.experimental import pallas as pl
from jax.experimental.pallas import tpu as pltpu

F32 = jnp.float32
BF16 = jnp.bfloat16
I32 = jnp.int32
U32 = jnp.uint32

D_MODEL = 1024
SEQ = 2048
HEAD_DIM = 64
N_Q_HEADS = 16
N_KV_HEADS = 2
ATTN_BLOCK = 128
WINDOW = 128
NUM_BUCKETS = 32
MAX_DISTANCE = 128
D_INNER = 2048
N_SSM_HEADS = 32
N_SSM_GROUPS = 4
D_STATE = 128
CONV_WIDTH = 4
CHUNK = 128
N_EXPERT_GROUPS = 4
EXPERTS_PER_GROUP = 8
N_EXPERTS = 32
D_EXPERT = 512
DEEPNORM_ALPHA = 2.0 ** 0.25
LN_EPS = 1e-5

LANES = 128
SUBLANES = 8

COL_Z = 0
COL_XS = 2048
COL_B = 4096
COL_C = 4608
COL_Q = 5120
COL_GA = 6144
COL_GS = 7168
COL_KV = 8192
N_PROJ = 8704

TM_PROJ = 512
CH_PROJ = 256
CONV_CH = D_INNER + 2 * N_SSM_GROUPS * D_STATE
RAW_SLOTS = 3
Q_BLOCKS_ATTN = 4
CHUNKS_SSD = 4
TM_MIX = 256
TM_FIN = 512
BM_MOE = 512
HALF_MOE = 256
PARTS_MOE = BM_MOE // HALF_MOE
TD_DISPATCH = 512
DMA_PRIORITIES = 2

VMEM_LIMIT = 56 * 1024 * 1024


def _layer_norm(x, g, b):
    mu = jnp.mean(x, axis=-1, keepdims=True)
    xc = x - mu
    var = jnp.mean(xc * xc, axis=-1, keepdims=True)
    return xc * lax.rsqrt(var + LN_EPS) * g + b


assert D_MODEL == SUBLANES * LANES


def _store_token_tiles(ref, val, n):
    for c in range(SUBLANES):
        ref[pl.ds(c, n, stride=SUBLANES), :] = val[:, c * LANES:(c + 1) * LANES]


def _load_token_tiles(ref, n):
    return jnp.concatenate([ref[pl.ds(c, n, stride=SUBLANES), :] for c in range(SUBLANES)], axis=1)


def _softplus(x):
    return jnp.maximum(x, 0.0) + jnp.log1p(jnp.exp(-jnp.abs(x)))


def _silu(x):
    return x * jax.nn.sigmoid(x)


def _ln_inproj_kernel(x_ref, g_ref, b_ref, w_hbm, wdt_ref, cw_ref, cb_ref, bg_ref, dtb_ref,
                      o_ref, dt_ref, w_scr, carry_scr, raw_scr, sem):
    i = pl.program_id(0)

    @pl.when(i == 0)
    def _():
        cp = pltpu.make_async_copy(w_hbm, w_scr, sem)
        cp.start()
        carry_scr[...] = jnp.zeros_like(carry_scr)
        cp.wait()

    hb = _layer_norm(x_ref[...], g_ref[...], b_ref[...]).astype(BF16)
    tm = hb.shape[0]

    seq_start = lax.rem(i, SEQ // TM_PROJ) == 0
    sub = lax.broadcasted_iota(I32, (tm // SUBLANES, SUBLANES, CH_PROJ), 1)

    def conv_silu(r, c0):
        cc = slice(c0 - COL_XS, c0 - COL_XS + CH_PROJ)
        prev = jnp.where(seq_start, 0.0, carry_scr[:, cc])
        carry_scr[:, cc] = r[tm - SUBLANES:]
        acc = cb_ref[:, cc] + cw_ref[CONV_WIDTH - 1:CONV_WIDTH, cc] * r
        r3 = jnp.concatenate([prev, r], axis=0).reshape(tm // SUBLANES + 1, SUBLANES, CH_PROJ)
        for k in range(1, CONV_WIDTH):
            rot = pltpu.roll(r3, k, axis=1)
            rk = jnp.where(sub < k, rot[:-1], rot[1:]).reshape(tm, CH_PROJ)
            acc = acc + cw_ref[CONV_WIDTH - 1 - k:CONV_WIDTH - k, cc] * rk
        return _silu(acc)

    def epilogue(r, c0):
        if c0 < COL_XS:
            return _silu(r)
        if c0 < COL_Q:
            return conv_silu(r, c0)
        if COL_GA <= c0 < COL_KV:
            return jax.nn.sigmoid(r + bg_ref[:, c0 - COL_GA:c0 - COL_GA + CH_PROJ])
        return r

    def proj(c0):
        return jnp.dot(hb, w_scr[:, c0:c0 + CH_PROJ], preferred_element_type=F32)

    heavy = list(range(COL_XS, COL_Q, CH_PROJ))
    light = [c for c in range(0, N_PROJ, CH_PROJ) if c not in heavy]
    starts = []
    while heavy or light:
        starts += heavy[:1] + light[:2]
        heavy, light = heavy[1:], light[2:]
    zero = jnp.minimum(i, 0)
    ahead = RAW_SLOTS - 1
    for n in range(ahead):
        raw_scr[zero + n] = proj(starts[n])
    for n, c0 in enumerate(starts):
        if n + ahead < len(starts):
            raw_scr[zero + (n + ahead) % RAW_SLOTS] = proj(starts[n + ahead])
        o_ref[:, c0:c0 + CH_PROJ] = epilogue(raw_scr[zero + n % RAW_SLOTS], c0).astype(BF16)
    dt_ref[...] = _softplus(jnp.dot(hb, wdt_ref[...], preferred_element_type=F32) + dtb_ref[...])


def _ln_inproj(x2, g, b, w_cat, w_dt, conv_w, conv_b, b_gate, dt_bias):
    t = x2.shape[0]
    const = lambda i: (0, 0)
    return pl.pallas_call(
        _ln_inproj_kernel,
        grid=(t // TM_PROJ,),
        in_specs=[
            pl.BlockSpec((TM_PROJ, D_MODEL), lambda i: (i, 0)),
            pl.BlockSpec((1, D_MODEL), const),
            pl.BlockSpec((1, D_MODEL), const),
            pl.BlockSpec(memory_space=pl.ANY),
            pl.BlockSpec((D_MODEL, LANES), const),
            pl.BlockSpec((CONV_WIDTH, CONV_CH), const),
            pl.BlockSpec((1, CONV_CH), const),
            pl.BlockSpec((1, 2 * D_MODEL), const),
            pl.BlockSpec((1, LANES), const),
        ],
        out_specs=[
            pl.BlockSpec((TM_PROJ, N_PROJ), lambda i: (i, 0)),
            pl.BlockSpec((TM_PROJ, LANES), lambda i: (i, 0)),
        ],
        out_shape=[
            jax.ShapeDtypeStruct((t, N_PROJ), BF16),
            jax.ShapeDtypeStruct((t, LANES), F32),
        ],
        scratch_shapes=[
            pltpu.VMEM((D_MODEL, N_PROJ), BF16),
            pltpu.VMEM((SUBLANES, CONV_CH), F32),
            pltpu.VMEM((RAW_SLOTS, TM_PROJ, CH_PROJ), F32),
            pltpu.SemaphoreType.DMA(()),
        ],
        compiler_params=pltpu.CompilerParams(
            dimension_semantics=("arbitrary",), vmem_limit_bytes=VMEM_LIMIT),
        name="ln_inproj",
    )(x2, g, b, w_cat, w_dt, conv_w, conv_b, b_gate, dt_bias)


def _attn_kernel(sink_ref, q_ref, kvc_ref, kvp_ref, bias_ref, o_ref):
    first = jnp.where(pl.program_id(1) == 0, 1, 0)
    lo = lax.broadcasted_iota(I32, (ATTN_BLOCK, LANES), 1) < HEAD_DIM
    for qb in range(Q_BLOCKS_ATTN):
        rows = slice(qb * ATTN_BLOCK, (qb + 1) * ATTN_BLOCK)
        for kv in range(N_KV_HEADS):
            ks = slice(kv * LANES, (kv + 1) * LANES)
            vs = slice((N_KV_HEADS + kv) * LANES, (N_KV_HEADS + kv + 1) * LANES)
            if qb == 0:
                k_prev, v_prev, table = kvp_ref[:, ks], kvp_ref[:, vs], first
            else:
                before = slice((qb - 1) * ATTN_BLOCK, qb * ATTN_BLOCK)
                k_prev, v_prev, table = kvc_ref[before, ks], kvc_ref[before, vs], 0
            kcat = jnp.concatenate([k_prev, kvc_ref[rows, ks]], axis=0)
            vcat = jnp.concatenate([v_prev, kvc_ref[rows, vs]], axis=0)
            for pp in range(4):
                p = kv * 4 + pp
                qp = q_ref[rows, p * LANES:(p + 1) * LANES].astype(F32)
                qm = jnp.concatenate([jnp.where(lo, qp, 0.0), jnp.where(lo, 0.0, qp)], axis=0).astype(BF16)
                s = lax.dot_general(qm, kcat, (((1,), (1,)), ((), ())), preferred_element_type=F32)
                outs = []
                for hh in range(2):
                    h = 2 * p + hh
                    sh = s[hh * ATTN_BLOCK:(hh + 1) * ATTN_BLOCK] + bias_ref[table, h]
                    snk = sink_ref[h]
                    m = jnp.maximum(jnp.max(sh, axis=-1, keepdims=True), snk)
                    pe = jnp.exp(sh - m)
                    den = jnp.sum(pe, axis=-1, keepdims=True) + jnp.exp(snk - m)
                    o = jnp.dot(pe.astype(BF16), vcat, preferred_element_type=F32)
                    outs.append(o / den)
                o_ref[rows, p * LANES:(p + 1) * LANES] = jnp.where(lo, outs[0], outs[1]).astype(BF16)


def _swa_attn(proj, bias2, sink, nbatch):
    t = proj.shape[0]
    rows_step = Q_BLOCKS_ATTN * ATTN_BLOCK
    nb = SEQ // rows_step
    q_blk = COL_Q // D_MODEL
    kv_blk = COL_KV // 512
    return pl.pallas_call(
        _attn_kernel,
        grid=(nbatch, nb),
        in_specs=[
            pl.BlockSpec(memory_space=pltpu.SMEM),
            pl.BlockSpec((rows_step, D_MODEL), lambda b, i: (b * nb + i, q_blk)),
            pl.BlockSpec((rows_step, 512), lambda b, i: (b * nb + i, kv_blk)),
            pl.BlockSpec((ATTN_BLOCK, 512),
                         lambda b, i: ((b * nb + i) * Q_BLOCKS_ATTN - jnp.minimum(i, 1), kv_blk)),
            pl.BlockSpec((2, N_Q_HEADS, ATTN_BLOCK, 2 * ATTN_BLOCK), lambda b, i: (0, 0, 0, 0)),
        ],
        out_specs=pl.BlockSpec((rows_step, D_MODEL), lambda b, i: (b * nb + i, 0)),
        out_shape=jax.ShapeDtypeStruct((t, D_MODEL), BF16),
        compiler_params=pltpu.CompilerParams(
            dimension_semantics=("arbitrary", "arbitrary"), vmem_limit_bytes=VMEM_LIMIT),
        name="swa_attn",
    )(sink, proj, proj, proj, bias2)


def _ssd_kernel(zs_ref, xs_ref, bc_ref, dt_ref, a_ref, dsk_ref, ng_ref, y_ref, state_scr):
    @pl.when(pl.program_id(1) == 0)
    def _():
        state_scr[...] = jnp.zeros_like(state_scr)

    chunks = [pl.ds(ck * CHUNK, CHUNK) for ck in range(CHUNKS_SSD)]
    terms = [_ssd_decay_terms(dt_ref.at[rows, :], a_ref) for rows in chunks]
    for rows, tm in zip(chunks, terms):
        _ssd_chunk(zs_ref.at[rows, :], xs_ref.at[rows, :], bc_ref.at[rows, :], tm,
                   dsk_ref, ng_ref, y_ref.at[rows, :], state_scr)


def _ssd_decay_terms(dt_ref, a_ref):
    dt = dt_ref[...]
    da = dt * a_ref[...]
    row = lax.broadcasted_iota(I32, (CHUNK, CHUNK), 0)
    col = lax.broadcasted_iota(I32, (CHUNK, CHUNK), 1)
    tril = jnp.where(row >= col, 1.0, 0.0).astype(BF16)
    hi = da.astype(BF16)
    r1 = da - hi.astype(F32)
    mid = r1.astype(BF16)
    low = (r1 - mid.astype(F32)).astype(BF16)
    acum = (jnp.dot(tril, hi, preferred_element_type=F32) + jnp.dot(tril, mid, preferred_element_type=F32)
            + jnp.dot(tril, low, preferred_element_type=F32))
    dtw = dt * jnp.exp(acum[CHUNK - 1:CHUNK, :] - acum)
    return acum, acum.T, dt.T, dtw


def _ssd_chunk(zs_ref, xs_ref, bc_ref, terms, dsk_ref, ng_ref, y_ref, state_scr):
    bm = bc_ref[:, :N_SSM_GROUPS * D_STATE]
    cm = bc_ref[:, N_SSM_GROUPS * D_STATE:]
    acum, acum_t, dt_t, dtw = terms
    causal = lax.broadcasted_iota(I32, (CHUNK, CHUNK), 0) >= lax.broadcasted_iota(I32, (CHUNK, CHUNK), 1)

    def over_lanes(v, e):
        return jnp.broadcast_to(v[:, e:e + 1], (CHUNK, LANES))

    def over_rows(v, e):
        return jnp.broadcast_to(v[e:e + 1, :], (CHUNK, LANES))

    lo = lax.broadcasted_iota(I32, (CHUNK, LANES), 1) < (LANES // 2)
    gw = D_INNER // N_SSM_GROUPS
    pairs = gw // LANES
    for g in range(N_SSM_GROUPS):
        c_g = cm[:, g * D_STATE:(g + 1) * D_STATE]
        b_g = bm[:, g * D_STATE:(g + 1) * D_STATE]
        cb = lax.dot_general(c_g, b_g, (((1,), (1,)), ((), ())), preferred_element_type=F32)
        cb = jnp.where(causal, cb, 0.0)
        st_g = state_scr[:, g * gw:(g + 1) * gw]
        y_off = jnp.dot(c_g, st_g.astype(BF16), preferred_element_type=F32)
        hg, xw, ea_end = [], [], []
        for jj in range(pairs):
            j = g * pairs + jj
            cs = slice(j * LANES, (j + 1) * LANES)
            xp_bf = xs_ref[:, cs]
            xp = xp_bf.astype(F32)
            ms, ea, wl = [], [], []
            for e in (2 * j, 2 * j + 1):
                a_col = over_lanes(acum, e)
                dec = jnp.exp(jnp.minimum(a_col - over_rows(acum_t, e), 0.0))
                ms.append((cb * dec * over_rows(dt_t, e)).astype(BF16))
                ea.append(jnp.exp(a_col))
                wl.append(over_lanes(dtw, e))
            zero = jnp.zeros_like(xp_bf)
            rhs = jnp.concatenate([jnp.where(lo, xp_bf, zero), jnp.where(lo, zero, xp_bf)], axis=0)
            y_diag = jnp.dot(jnp.concatenate(ms, axis=1), rhs, preferred_element_type=F32)
            ea_p = jnp.where(lo, ea[0], ea[1])
            y_p = y_diag + y_off[:, jj * LANES:(jj + 1) * LANES] * ea_p + dsk_ref[:, cs] * xp
            hg.append(y_p * zs_ref[:, cs].astype(F32))
            xw.append((xp * jnp.where(lo, wl[0], wl[1])).astype(BF16))
            ea_end.append(ea_p[CHUNK - 1:CHUNK, :])
        b_t = b_g.astype(F32).T.astype(BF16)
        upd = jnp.dot(b_t, jnp.concatenate(xw, axis=1), preferred_element_type=F32)
        state_scr[:, g * gw:(g + 1) * gw] = st_g * jnp.concatenate(ea_end, axis=1) + upd
        h_g = jnp.concatenate(hg, axis=1)
        ms_g = jnp.mean(h_g * h_g, axis=-1, keepdims=True)
        y_ref[:, g * gw:(g + 1) * gw] = (h_g * lax.rsqrt(ms_g + LN_EPS) * ng_ref[:, g * gw:(g + 1) * gw]).astype(BF16)


def _ssd(proj, dt, a_neg, dskip_b, norm_g, nbatch):
    t = proj.shape[0]
    rows_step = CHUNKS_SSD * CHUNK
    nc = SEQ // rows_step
    const = lambda b, c: (0, 0)
    return pl.pallas_call(
        _ssd_kernel,
        grid=(nbatch, nc),
        in_specs=[
            pl.BlockSpec((rows_step, D_INNER), lambda b, c: (b * nc + c, COL_Z // D_INNER)),
            pl.BlockSpec((rows_step, D_INNER), lambda b, c: (b * nc + c, COL_XS // D_INNER)),
            pl.BlockSpec((rows_step, 1024), lambda b, c: (b * nc + c, COL_B // 1024)),
            pl.BlockSpec((rows_step, LANES), lambda b, c: (b * nc + c, 0)),
            pl.BlockSpec((1, LANES), const),
            pl.BlockSpec((1, D_INNER), const),
            pl.BlockSpec((1, D_INNER), const),
        ],
        out_specs=pl.BlockSpec((rows_step, D_INNER), lambda b, c: (b * nc + c, 0)),
        out_shape=jax.ShapeDtypeStruct((t, D_INNER), BF16),
        scratch_shapes=[pltpu.VMEM((D_STATE, D_INNER), F32)],
        compiler_params=pltpu.CompilerParams(
            dimension_semantics=("arbitrary", "arbitrary"), vmem_limit_bytes=VMEM_LIMIT),
        name="ssd",
    )(proj, proj, proj, dt, a_neg, dskip_b, norm_g)


def _mix_route_kernel(attn_ref, y_ref, gate_ref, x_ref, lng_ref, lnb_ref, wao_ref, wso_ref, wo_ref,
                      l1g_ref, l1b_ref, wr_ref, h1t_ref, eid_ref, gw_ref, cnt_ref, mix_scr, stage_scr, logit_scr):
    i = pl.program_id(0)

    @pl.when(i == 0)
    def _():
        cnt_ref[...] = jnp.zeros_like(cnt_ref)
        mix_scr[...] = jnp.zeros_like(mix_scr)
        logit_scr[...] = jnp.zeros_like(logit_scr)

    logits = logit_scr[...]
    lane = lax.broadcasted_iota(I32, (TM_MIX, LANES), 1)
    neg = -jnp.inf
    lg = jnp.where(lane < N_EXPERT_GROUPS, logits, neg)
    mg = jnp.max(lg, axis=-1, keepdims=True)
    grp = jnp.min(jnp.where(lg == mg, lane, LANES), axis=-1, keepdims=True)
    p_grp = 1.0 / jnp.sum(jnp.exp(lg - mg), axis=-1, keepdims=True)
    el = lane - N_EXPERT_GROUPS
    in_grp = (el >= 0) & (el < N_EXPERTS) & (lax.shift_right_arithmetic(el, 3) == grp)
    le = jnp.where(in_grp, logits, neg)
    m1 = jnp.max(le, axis=-1, keepdims=True)
    i1 = jnp.min(jnp.where(le == m1, lane, LANES), axis=-1, keepdims=True)
    le2 = jnp.where(lane == i1, neg, le)
    m2 = jnp.max(le2, axis=-1, keepdims=True)
    i2 = jnp.min(jnp.where(le2 == m2, lane, LANES), axis=-1, keepdims=True)
    e2 = jnp.exp(m2 - m1)
    p1 = p_grp / (1.0 + e2)
    p2 = p1 * e2
    gw_ref[...] = jnp.where(lane == 0, p1, jnp.where(lane == 1, p2, 0.0))

    pick = (lane == i1) | (lane == i2)
    row = lax.broadcasted_iota(I32, (TM_MIX, LANES), 0)
    ones = jnp.where(pick, 1.0, 0.0)
    seen = ones
    shift = 1
    while shift < TM_MIX:
        seen = seen + jnp.where(row >= shift, pltpu.roll(seen, shift, axis=0), 0.0)
        shift *= 2
    before = seen - ones + cnt_ref[...].astype(F32)
    rank1 = jnp.sum(jnp.where(lane == i1, before, 0.0), axis=-1, keepdims=True).astype(I32)
    rank2 = jnp.sum(jnp.where(lane == i2, before, 0.0), axis=-1, keepdims=True).astype(I32)
    eid_ref[...] = jnp.where(lane == 0, i1 - N_EXPERT_GROUPS,
                             jnp.where(lane == 1, i2 - N_EXPERT_GROUPS,
                                       jnp.where(lane == 2, rank1, jnp.where(lane == 3, rank2, 0))))
    picked = jnp.where(pick & (i > 1), 1, 0)
    cnt_ref[...] += jnp.sum(picked, axis=0, keepdims=True)

    slot = lax.rem(i, 2)
    mix_prev = mix_scr[1 - slot]
    zero = jnp.minimum(i, 0)
    stage_scr[zero] = jnp.dot(attn_ref[...], wao_ref[...], preferred_element_type=F32)
    stage_scr[zero + 1] = jnp.dot(y_ref[...], wso_ref[...], preferred_element_type=F32)
    gate = gate_ref[...].astype(F32)
    mixin = (gate[:, :D_MODEL] * stage_scr[zero] + gate[:, D_MODEL:] * stage_scr[zero + 1]).astype(BF16)
    mix_scr[slot] = jnp.dot(mixin, wo_ref[...], preferred_element_type=F32)

    h = _layer_norm(x_ref[...], lng_ref[...], lnb_ref[...])
    h1 = _layer_norm(DEEPNORM_ALPHA * h + mix_prev, l1g_ref[...], l1b_ref[...])
    _store_token_tiles(h1t_ref, h1, TM_MIX)
    logit_scr[...] = jnp.dot(h1.astype(BF16), wr_ref[...], preferred_element_type=F32)


def _mix_route(attn, yssm, proj, x2, lng, lnb, w_ao, w_so, w_o, l1g, l1b, w_r):
    t = x2.shape[0]
    nt = t // TM_MIX
    const = lambda i: (0, 0)
    row = lambda i: (jnp.minimum(i, nt - 1), 0)
    prev = lambda i: (jnp.clip(i - 1, 0, nt - 1), 0)
    prev2 = lambda i: (jnp.maximum(i - 2, 0), 0)
    return pl.pallas_call(
        _mix_route_kernel,
        grid=(nt + 2,),
        in_specs=[
            pl.BlockSpec((TM_MIX, D_MODEL), row),
            pl.BlockSpec((TM_MIX, D_INNER), row),
            pl.BlockSpec((TM_MIX, 2 * D_MODEL), lambda i: (jnp.minimum(i, nt - 1), COL_GA // (2 * D_MODEL))),
            pl.BlockSpec((TM_MIX, D_MODEL), prev),
            pl.BlockSpec((1, D_MODEL), const),
            pl.BlockSpec((1, D_MODEL), const),
            pl.BlockSpec((D_MODEL, D_MODEL), const),
            pl.BlockSpec((D_INNER, D_MODEL), const),
            pl.BlockSpec((D_MODEL, D_MODEL), const),
            pl.BlockSpec((1, D_MODEL), const),
            pl.BlockSpec((1, D_MODEL), const),
            pl.BlockSpec((D_MODEL, LANES), const),
        ],
        out_specs=[
            pl.BlockSpec((TM_MIX * SUBLANES, LANES), prev),
            pl.BlockSpec((TM_MIX, LANES), prev2),
            pl.BlockSpec((TM_MIX, LANES), prev2),
            pl.BlockSpec((1, LANES), const),
        ],
        out_shape=[
            jax.ShapeDtypeStruct((t * SUBLANES, LANES), F32),
            jax.ShapeDtypeStruct((t, LANES), I32),
            jax.ShapeDtypeStruct((t, LANES), F32),
            jax.ShapeDtypeStruct((1, LANES), I32),
        ],
        scratch_shapes=[pltpu.VMEM((2, TM_MIX, D_MODEL), F32), pltpu.VMEM((2, TM_MIX, D_MODEL), F32),
                        pltpu.VMEM((TM_MIX, LANES), F32)],
        compiler_params=pltpu.CompilerParams(
            dimension_semantics=("arbitrary",), vmem_limit_bytes=VMEM_LIMIT),
        name="mix_route",
    )(attn, yssm, proj, x2, lng, lnb, w_ao, w_so, w_o, l1g, l1b, w_r)


def _dispatch_kernel(dest_ref, h_ref, z_ref, xr_hbm, sem, *, n_token_steps):
    def copy_all(src_ref):
        for j in range(TD_DISPATCH):
            for k in range(2):
                r8 = pl.multiple_of(dest_ref[0, 0, 2 * j + k], SUBLANES)
                pltpu.make_async_copy(src_ref.at[pl.ds(j * SUBLANES, SUBLANES), :],
                                      xr_hbm.at[pl.ds(r8, SUBLANES), :], sem).start(priority=k)

    @pl.when(pl.program_id(0) < n_token_steps)
    def _():
        copy_all(h_ref)

    @pl.when(pl.program_id(0) >= n_token_steps)
    def _():
        copy_all(z_ref)

    for k in range(2):
        pltpu.make_async_copy(h_ref, xr_hbm.at[pl.ds(0, TD_DISPATCH * SUBLANES), :], sem).wait()


def _dispatch(dest8, h1t):
    n_rows = dest8.shape[0]
    nt = h1t.shape[0] // (TD_DISPATCH * SUBLANES)
    n_steps = n_rows // (2 * TD_DISPATCH)
    zeros = jnp.zeros((TD_DISPATCH * SUBLANES, LANES), F32)
    return pl.pallas_call(
        functools.partial(_dispatch_kernel, n_token_steps=nt),
        grid=(n_steps,),
        in_specs=[
            pl.BlockSpec((1, 1, 2 * TD_DISPATCH), lambda i: (i, 0, 0), memory_space=pltpu.SMEM),
            pl.BlockSpec((TD_DISPATCH * SUBLANES, LANES), lambda i: (jnp.minimum(i, nt - 1), 0)),
            pl.BlockSpec((TD_DISPATCH * SUBLANES, LANES), lambda i: (0, 0)),
        ],
        out_specs=pl.BlockSpec(memory_space=pl.ANY),
        out_shape=jax.ShapeDtypeStruct((n_rows * SUBLANES, LANES), F32),
        scratch_shapes=[pltpu.SemaphoreType.DMA(())],
        compiler_params=pltpu.CompilerParams(
            dimension_semantics=("arbitrary",), vmem_limit_bytes=VMEM_LIMIT),
        name="dispatch",
    )(dest8.reshape(n_steps, 1, 2 * TD_DISPATCH), h1t, zeros)


def _moe_kernel(be_ref, dst_ref, x_ref, wg_ref, wu_ref, wd_ref, y_hbm, ybuf, wg_bf, wu_bf, wd_bf, ssem):
    i = pl.program_id(0)
    nblk = pl.num_programs(0)
    part_rows = HALF_MOE * SUBLANES

    def issue_scatter(s):
        for j in range(HALF_MOE):
            r8 = pl.multiple_of(dst_ref[0, 0, s * HALF_MOE + j], SUBLANES)
            pltpu.make_async_copy(ybuf.at[s, pl.ds(j * SUBLANES, SUBLANES), :], y_hbm.at[pl.ds(r8, SUBLANES), :],
                                  ssem.at[s]).start(priority=j % DMA_PRIORITIES)

    def wait_scatter(s):
        pltpu.make_async_copy(ybuf.at[s], y_hbm.at[pl.ds(0, part_rows), :], ssem.at[s]).wait()

    @pl.when((i == 0) | (be_ref[i] != be_ref[jnp.maximum(i - 1, 0)]))
    def _():
        wg_bf[...] = wg_ref[0].astype(BF16)
        wu_bf[...] = wu_ref[0].astype(BF16)
        wd_bf[...] = wd_ref[0].astype(BF16)

    @pl.when(i > 0)
    def _():
        for s in range(PARTS_MOE):
            wait_scatter(s)

    for s in range(PARTS_MOE):
        x = _load_token_tiles(x_ref.at[pl.ds(s * part_rows, part_rows), :], HALF_MOE).astype(BF16)
        gt = jnp.dot(x, wg_bf[...], preferred_element_type=F32)
        up = jnp.dot(x, wu_bf[...], preferred_element_type=F32)
        hid = (gt * jax.nn.sigmoid(gt) * up).astype(BF16)
        y = jnp.dot(hid, wd_bf[...], preferred_element_type=F32)
        _store_token_tiles(ybuf.at[s], y, HALF_MOE)
    for s in range(PARTS_MOE):
        issue_scatter(s)

    @pl.when(i == nblk - 1)
    def _():
        for s in range(PARTS_MOE):
            wait_scatter(s)


def _moe(block_e, dst8, xr, w_gate, w_up, w_down, n_y_tiles):
    nblk = dst8.shape[0] // BM_MOE
    grid_spec = pltpu.PrefetchScalarGridSpec(
        num_scalar_prefetch=1,
        grid=(nblk,),
        in_specs=[
            pl.BlockSpec((1, 1, BM_MOE), lambda i, be: (i, 0, 0), memory_space=pltpu.SMEM),
            pl.BlockSpec((BM_MOE * SUBLANES, LANES), lambda i, be: (i, 0)),
            pl.BlockSpec((1, D_MODEL, D_EXPERT), lambda i, be: (be[i], 0, 0)),
            pl.BlockSpec((1, D_MODEL, D_EXPERT), lambda i, be: (be[i], 0, 0)),
            pl.BlockSpec((1, D_EXPERT, D_MODEL), lambda i, be: (be[i], 0, 0)),
        ],
        out_specs=pl.BlockSpec(memory_space=pl.ANY),
        scratch_shapes=[
            pltpu.VMEM((PARTS_MOE, HALF_MOE * SUBLANES, LANES), F32),
            pltpu.VMEM((D_MODEL, D_EXPERT), BF16),
            pltpu.VMEM((D_MODEL, D_EXPERT), BF16),
            pltpu.VMEM((D_EXPERT, D_MODEL), BF16),
            pltpu.SemaphoreType.DMA((PARTS_MOE,)),
        ],
    )
    return pl.pallas_call(
        _moe_kernel,
        grid_spec=grid_spec,
        out_shape=jax.ShapeDtypeStruct((n_y_tiles * SUBLANES, LANES), F32),
        compiler_params=pltpu.CompilerParams(
            dimension_semantics=("arbitrary",), vmem_limit_bytes=VMEM_LIMIT),
        name="moe",
    )(block_e, dst8.reshape(nblk, 1, BM_MOE), xr, w_gate, w_up, w_down)


def _final_kernel(h1t_ref, y0_ref, y1_ref, gw_ref, g_ref, b_ref, o_ref):
    n = o_ref.shape[0]
    gw = gw_ref[...]
    ffn = gw[:, 0:1] * _load_token_tiles(y0_ref, n) + gw[:, 1:2] * _load_token_tiles(y1_ref, n)
    o_ref[...] = _layer_norm(DEEPNORM_ALPHA * _load_token_tiles(h1t_ref, n) + ffn, g_ref[...], b_ref[...])


def _final_ln(h1t, y, gw, g, b):
    t = gw.shape[0]
    nt = t // TM_FIN
    const = lambda i: (0, 0)
    tiles = (TM_FIN * SUBLANES, LANES)
    return pl.pallas_call(
        _final_kernel,
        grid=(nt,),
        in_specs=[
            pl.BlockSpec(tiles, lambda i: (i, 0)),
            pl.BlockSpec(tiles, lambda i: (i, 0)),
            pl.BlockSpec(tiles, lambda i: (i + nt, 0)),
            pl.BlockSpec((TM_FIN, LANES), lambda i: (i, 0)),
            pl.BlockSpec((1, D_MODEL), const),
            pl.BlockSpec((1, D_MODEL), const),
        ],
        out_specs=pl.BlockSpec((TM_FIN, D_MODEL), lambda i: (i, 0)),
        out_shape=jax.ShapeDtypeStruct((t, D_MODEL), F32),
        compiler_params=pltpu.CompilerParams(
            dimension_semantics=("arbitrary",), vmem_limit_bytes=VMEM_LIMIT),
        name="final_ln",
    )(h1t, y, y, gw, g, b)


def _t5_causal_bucket(dist):
    max_exact = NUM_BUCKETS // 2
    d_f = jnp.maximum(dist, 1).astype(F32)
    large = max_exact + (jnp.log(d_f / max_exact) / math.log(MAX_DISTANCE / max_exact)
                         * (NUM_BUCKETS - max_exact)).astype(I32)
    large = jnp.minimum(large, NUM_BUCKETS - 1)
    return jnp.where(dist < max_exact, dist, large)


def _attn_bias_tables(rel_bias):
    qi = jnp.arange(ATTN_BLOCK)[:, None]
    kj = jnp.arange(2 * ATTN_BLOCK)[None, :]
    dist = qi + ATTN_BLOCK - kj
    bucket = _t5_causal_bucket(jnp.clip(dist, 0, None))
    hit = bucket[None, None] == jnp.arange(NUM_BUCKETS, dtype=I32)[None, :, None, None]
    bias = jnp.sum(jnp.where(hit, rel_bias.astype(F32).T[:, :, None, None], 0.0), axis=1)
    in_window = (dist >= 0) & (dist < WINDOW)
    normal = jnp.where(in_window[None], bias, -jnp.inf)
    first = jnp.where((in_window & (kj >= ATTN_BLOCK))[None], bias, -jnp.inf)
    return jnp.stack([normal, first], axis=0)


ASSIGN_BITS = 17


def _route_metadata(eid, rank, counts, t):
    n_assign = 2 * t
    assert n_assign <= 1 << ASSIGN_BITS
    n_blk = n_assign // BM_MOE + N_EXPERTS
    padded = (counts + BM_MOE - 1) // BM_MOE * BM_MOE
    pends = jnp.cumsum(padded)
    n_pad = padded - counts
    experts = jnp.arange(N_EXPERTS, dtype=I32)
    start_of = jnp.sum(jnp.where(eid[..., None] == experts, pends - padded, 0), axis=-1)
    j = jnp.arange(BM_MOE, dtype=I32)[None, :]
    ee = jnp.arange(N_EXPERTS, dtype=I32)[:, None]
    is_tail = (j < n_pad[:, None]).reshape(-1)
    tail_row = ((pends - n_pad)[:, None] + j).reshape(-1)
    spare_row = pends[-1] + jnp.cumsum(jnp.where(is_tail, 0, 1)) - 1
    pad_rows = jnp.where(is_tail, tail_row, spare_row)
    dest8 = (jnp.concatenate([(start_of + rank).reshape(-1), pad_rows]) * SUBLANES).astype(I32)
    key = (eid.reshape(-1) << (ASSIGN_BITS + 1)) | jnp.arange(n_assign, dtype=I32)
    pad_key = jnp.where(j < n_pad[:, None], (ee << (ASSIGN_BITS + 1)) | (1 << ASSIGN_BITS) | j,
                        jnp.iinfo(jnp.int32).max)
    keys = jnp.sort(jnp.concatenate([key, pad_key.reshape(-1)]))
    real = (keys & (1 << ASSIGN_BITS)) == 0
    a = keys & ((1 << ASSIGN_BITS) - 1)
    tok = a >> 1
    r = jnp.arange(n_blk * BM_MOE, dtype=I32)
    dst8 = jnp.where(real, (a & 1) * t + tok, n_assign + (r % BM_MOE)) * SUBLANES
    blk_start = jnp.arange(n_blk, dtype=I32)[:, None] * BM_MOE
    block_e = jnp.minimum(jnp.sum((pends[None, :] <= blk_start).astype(I32), axis=1), N_EXPERTS - 1)
    return block_e.astype(I32), dest8, dst8.astype(I32), n_assign + BM_MOE


def kernel(x, ln_in_g, ln_in_b, rel_bias, w_in, b_gate, attn_sink, conv_w, conv_b, dt_bias, a_log, d_skip,
           ssm_norm_g, w_attn_out, w_ssm_out, w_out, ln1_g, ln1_b, w_group_router, w_expert_router,
           w_gate_e, w_up_e, w_down_e, ln2_g, ln2_b):
    nbatch, seq, d = x.shape
    assert seq == SEQ and d == D_MODEL
    t = nbatch * seq
    x2 = x.reshape(t, d)
    l = 0

    w = w_in[l]
    q_w, k_w, v_w, z_w, xs_w, b_w, c_w, dt_w, ga_w, gs_w = jnp.split(
        w, [1024, 1152, 1280, 3328, 5376, 5888, 6400, 6432, 7456], axis=1)
    k0, k1 = k_w[:, :HEAD_DIM], k_w[:, HEAD_DIM:]
    v0, v1 = v_w[:, :HEAD_DIM], v_w[:, HEAD_DIM:]
    w_cat = jnp.concatenate([z_w, xs_w, b_w, c_w, q_w * (HEAD_DIM ** -0.5), ga_w, gs_w,
                             k0, k0, k1, k1, v0, v0, v1, v1], axis=1).astype(BF16)
    w_dt = jnp.pad(dt_w, ((0, 0), (0, LANES - N_SSM_HEADS))).astype(BF16)
    row = lambda v: v.reshape(1, -1).astype(F32)
    pad_heads = lambda v: jnp.pad(v.astype(F32), (0, LANES - N_SSM_HEADS)).reshape(1, LANES)
    a_neg = pad_heads(-jnp.exp(a_log[l].astype(F32)))
    dskip_b = jnp.repeat(d_skip[l].astype(F32), D_INNER // N_SSM_HEADS).reshape(1, D_INNER)
    w_r = jnp.pad(jnp.concatenate([w_group_router[l], w_expert_router[l]], axis=1),
                  ((0, 0), (0, LANES - N_EXPERT_GROUPS - N_EXPERTS))).astype(BF16)
    bias2 = _attn_bias_tables(rel_bias)

    proj, dt = _ln_inproj(x2, row(ln_in_g), row(ln_in_b), w_cat, w_dt, conv_w[l].astype(F32), row(conv_b[l]),
                          row(b_gate[l]), pad_heads(dt_bias[l]))
    attn = _swa_attn(proj, bias2, attn_sink[l].astype(F32), nbatch)
    yssm = _ssd(proj, dt, a_neg, dskip_b, row(ssm_norm_g[l]), nbatch)
    h1t, eid, gw, cnt = _mix_route(attn, yssm, proj, x2, row(ln_in_g), row(ln_in_b),
                                   w_attn_out[l].astype(BF16), w_ssm_out[l].astype(BF16), w_out[l].astype(BF16),
                                   row(ln1_g[l]), row(ln1_b[l]), w_r)

    counts = cnt[0, N_EXPERT_GROUPS:N_EXPERT_GROUPS + N_EXPERTS]
    block_e, dest8, dst8, n_y_tiles = _route_metadata(eid[:, :2], eid[:, 2:4], counts, t)
    xr = _dispatch(dest8, h1t)
    y = _moe(block_e, dst8, xr, w_gate_e[l], w_up_e[l], w_down_e[l], n_y_tiles)
    out = _final_ln(h1t, y, gw, row(ln2_g[l]), row(ln2_b[l]))
    return out.reshape(nbatch, seq, d)
```
